```python
import math
import numpy as np
import jax
import jax.numpy as jnp
from jax import lax

D_MODEL = 2048
BATCH = 4
SEQ = 2048
DEPTH = 1
DEC_BATCH = 2
DEC_SEQ = 8192
PAST_LEN = 128

GRID_W = 64
WIN_H = 8
WIN_W = 16
H_A = 8
DH_A = 128
H_M = 4
DH_M = 256
W_A = H_A * DH_A
W_M = H_M * DH_M
D_MIX = W_A + W_M
CHUNK = 64
D_FF = 5632
PLE_DIM = 256
EPS = 1e-6
IN_SIZES = (W_A, W_A, W_A, W_M, W_M, W_M, W_M, 2 * H_M, 2 * H_M)
D_IN = sum(IN_SIZES)

kernel_name = "hymba_natten_mlstm_macaron_encoder"


def rms_norm(x, g):
    xf = x.astype(jnp.float32)
    y = xf * lax.rsqrt(jnp.mean(xf * xf, axis=-1, keepdims=True) + EPS)
    return (y * g).astype(x.dtype)


def swiglu(h, wg, wu, wd):
    return (jax.nn.silu(h @ wg) * (h @ wu)) @ wd


def neighbourhood_attention(q, k, v, rpb):
    B, S, H, Dh = q.shape
    R = S // GRID_W
    KH = min(WIN_H, R)
    r = jnp.arange(R)
    rs = jnp.clip(r - KH // 2, 0, R - KH)
    rows = rs[:, None] + jnp.arange(KH)[None, :]
    c = jnp.arange(GRID_W)
    cs = jnp.clip(c - WIN_W // 2, 0, GRID_W - WIN_W)
    colmask = (c[None, :] >= cs[:, None]) & (c[None, :] < cs[:, None] + WIN_W)

    qg = q.reshape(B, R, GRID_W, H, Dh)
    kb = k.reshape(B, R, GRID_W, H, Dh)[:, rows]
    vb = v.reshape(B, R, GRID_W, H, Dh)[:, rows]

    s = jnp.einsum('brqhd,brikhd->brhqik', qg, kb).astype(jnp.float32) * (Dh ** -0.5)
    dr = rows - r[:, None] + (WIN_H - 1)
    dc = jnp.clip(c[None, :] - c[:, None], -(WIN_W - 1), WIN_W - 1) + (WIN_W - 1)
    bias = rpb[:, dr[:, None, :, None], dc[None, :, None, :]]
    s = s + jnp.transpose(bias, (1, 0, 2, 3, 4))[None].astype(jnp.float32)
    s = jnp.where(colmask[:, None, :], s, -jnp.inf)
    p = jax.nn.softmax(s.reshape(B, R, H, GRID_W, KH * GRID_W), axis=-1)
    p = p.reshape(B, R, H, GRID_W, KH, GRID_W).astype(v.dtype)
    out = jnp.einsum('brhqik,brikhd->brqhd', p, vb)
    return out.reshape(B, S, H * Dh)


def mlstm_chunkwise(q, k, v, ig, lf):
    B, H, S, Dh = q.shape
    L = CHUNK
    N = S // L
    q = q.reshape(B, H, N, L, Dh)
    k = k.reshape(B, H, N, L, Dh)
    v = v.reshape(B, H, N, L, Dh)
    ig = ig.reshape(B, H, N, L)
    b = jnp.cumsum(lf.reshape(B, H, N, L), axis=-1)
    a = b[..., -1]

    g = a[..., None] - b + ig
    m_loc = jnp.max(g, axis=-1)
    w = jnp.exp(g - m_loc[..., None])
    C_loc = jnp.einsum('bhns,bhnsd,bhnse->bhnde', w, v, k)
    n_loc = jnp.einsum('bhns,bhnse->bhne', w, k)

    def step(carry, xs):
        C, n, m = carry
        a_c, m_l, C_l, n_l = xs
        m_new = jnp.maximum(a_c + m, m_l)
        s_prev = jnp.exp(a_c + m - m_new)
        s_loc = jnp.exp(m_l - m_new)
        C_new = s_prev[..., None, None] * C + s_loc[..., None, None] * C_l
        n_new = s_prev[..., None] * n + s_loc[..., None] * n_l
        return (C_new, n_new, m_new), (C, n, m)

    init = (jnp.zeros((B, H, Dh, Dh), jnp.float32), jnp.zeros((B, H, Dh), jnp.float32),
            jnp.zeros((B, H), jnp.float32))
    xs = (jnp.moveaxis(a, 2, 0), jnp.moveaxis(m_loc, 2, 0), jnp.moveaxis(C_loc, 2, 0), jnp.moveaxis(n_loc, 2, 0))
    _, (C_prev, n_prev, m_prev) = lax.scan(step, init, xs)
    C_prev = jnp.moveaxis(C_prev, 0, 2)
    n_prev = jnp.moveaxis(n_prev, 0, 2)
    m_prev = jnp.moveaxis(m_prev, 0, 2)

    dmat = b[..., :, None] - b[..., None, :] + ig[..., None, :]
    tri = jnp.tril(jnp.ones((L, L), dtype=bool))
    dmat = jnp.where(tri, dmat, -jnp.inf)
    inter = b + m_prev[..., None]
    m_t = jnp.maximum(inter, jnp.max(dmat, axis=-1))
    sqk = jnp.einsum('bhntd,bhnsd->bhnts', q, k) * jnp.exp(dmat - m_t[..., None])
    sc = jnp.exp(inter - m_t)
    num = sc[..., None] * jnp.einsum('bhnde,bhnte->bhntd', C_prev, q) + jnp.einsum('bhnts,bhnsd->bhntd', sqk, v)
    den = sc * jnp.einsum('bhne,bhnte->bhnt', n_prev, q) + jnp.sum(sqk, axis=-1)
    h = num / jnp.maximum(jnp.abs(den), jnp.exp(-m_t))[..., None]
    return h.reshape(B, H, S, Dh)


def mlstm_bidirectional(q, k, v, ig_f, lf_f, ig_b, lf_b):
    hf = mlstm_chunkwise(q, k, v, ig_f, lf_f)
    fl = lambda t: jnp.flip(t, axis=2)
    hb = fl(mlstm_chunkwise(fl(q), fl(k), fl(v), fl(ig_b), fl(lf_b)))
    return hf + hb


def encoder_layer(x, pe, g_ffn1, w_ffn1_gate, w_ffn1_up, w_ffn1_down, g_mix, w_in, b_igate, b_fgate,
                  g_qn, g_kn, rpb, g_mh, w_out, g_ffn2, w_ffn2_gate, w_ffn2_up, w_ffn2_down,
                  g_ple, w_ple_gate, w_ple_proj):
    B, S, _ = x.shape
    x = x + 0.5 * swiglu(rms_norm(x, g_ffn1), w_ffn1_gate, w_ffn1_up, w_ffn1_down)

    h = rms_norm(x, g_mix)
    u = h @ w_in
    offs = np.cumsum(IN_SIZES)[:-1].tolist()
    qa, ka, va, qm, km, vm, og, gi, gf = jnp.split(u, offs, axis=-1)

    qa = rms_norm(qa.reshape(B, S, H_A, DH_A), g_qn)
    ka = rms_norm(ka.reshape(B, S, H_A, DH_A), g_kn)
    va = va.reshape(B, S, H_A, DH_A)
    ya = neighbourhood_attention(qa, ka, va, rpb)

    to_bhsd = lambda t: jnp.transpose(t.reshape(B, S, H_M, DH_M), (0, 2, 1, 3)).astype(jnp.float32)
    qm_ = to_bhsd(qm)
    km_ = to_bhsd(km) * (DH_M ** -0.5)
    vm_ = to_bhsd(vm)
    gi = jnp.transpose((gi.reshape(B, S, 2, H_M) + b_igate).astype(jnp.float32), (2, 0, 3, 1))
    lf = jax.nn.log_sigmoid(jnp.transpose((gf.reshape(B, S, 2, H_M) + b_fgate).astype(jnp.float32), (2, 0, 3, 1)))
    hm = mlstm_bidirectional(qm_, km_, vm_, gi[0], lf[0], gi[1], lf[1])
    hm = jnp.transpose(hm, (0, 2, 1, 3))
    hm = rms_norm(hm, g_mh.reshape(H_M, DH_M)).reshape(B, S, W_M)
    ym = (jax.nn.sigmoid(og.astype(jnp.float32)) * hm).astype(x.dtype)

    x = x + jnp.concatenate([ya.astype(x.dtype), ym], axis=-1) @ w_out

    x = x + 0.5 * swiglu(rms_norm(x, g_ffn2), w_ffn2_gate, w_ffn2_up, w_ffn2_down)

    gate = jax.nn.sigmoid(rms_norm(x, g_ple) @ w_ple_gate)
    x = x + gate * (pe @ w_ple_proj)
    return x


def run_trunk(x, p, g_ffn1, w_ffn1_gate, w_ffn1_up, w_ffn1_down, g_mix, w_in, b_igate, b_fgate,
              g_qn, g_kn, rpb, g_mh, w_out, g_ffn2, w_ffn2_gate, w_ffn2_up, w_ffn2_down,
              g_ple, w_ple_gate, w_ple_proj):
    for i in range(DEPTH):
        x = encoder_layer(x, p[i], g_ffn1[i], w_ffn1_gate[i], w_ffn1_up[i], w_ffn1_down[i], g_mix[i], w_in[i],
                          b_igate[i], b_fgate[i], g_qn[i], g_kn[i], rpb[i], g_mh[i], w_out[i], g_ffn2[i],
                          w_ffn2_gate[i], w_ffn2_up[i], w_ffn2_down[i], g_ple[i], w_ple_gate[i], w_ple_proj[i])
    return x


def setup_inputs(seed: int = 0) -> dict:
    key = jax.random.key(seed)
    ks = jax.random.split(key, 24)
    nrm = lambda k, shape, scale: jax.random.normal(k, shape, jnp.float32) * scale
    gain = lambda k, shape: 1.0 + 0.1 * jax.random.normal(k, shape, jnp.float32)
    L = DEPTH
    return {
        "x_prompt": nrm(ks[0], (BATCH, SEQ, D_MODEL), 1.0),
        "x_sample": nrm(ks[1], (DEC_BATCH, DEC_SEQ, D_MODEL), 1.0),
        "p_prompt": nrm(ks[2], (DEPTH, BATCH, SEQ, PLE_DIM), 1.0),
        "p_sample": nrm(ks[3], (DEPTH, DEC_BATCH, DEC_SEQ, PLE_DIM), 1.0),
        "g_ffn1": gain(ks[4], (L, D_MODEL)),
        "w_ffn1_gate": nrm(ks[5], (L, D_MODEL, D_FF), D_MODEL ** -0.5),
        "w_ffn1_up": nrm(ks[6], (L, D_MODEL, D_FF), D_MODEL ** -0.5),
        "w_ffn1_down": nrm(ks[7], (L, D_FF, D_MODEL), D_FF ** -0.5),
        "g_mix": gain(ks[8], (L, D_MODEL)),
        "w_in": nrm(ks[9], (L, D_MODEL, D_IN), D_MODEL ** -0.5),
        "b_igate": nrm(ks[10], (L, 2, H_M), 0.1),
        "b_fgate": 3.0 + nrm(ks[11], (L, 2, H_M), 0.5),
        "g_qn": gain(ks[12], (L, DH_A)),
        "g_kn": gain(ks[13], (L, DH_A)),
        "rpb": nrm(ks[14], (L, H_A, 2 * WIN_H - 1, 2 * WIN_W - 1), 0.5),
        "g_mh": gain(ks[15], (L, W_M)),
        "w_out": nrm(ks[16], (L, D_MIX, D_MODEL), D_MIX ** -0.5),
        "g_ffn2": gain(ks[17], (L, D_MODEL)),
        "w_ffn2_gate": nrm(ks[18], (L, D_MODEL, D_FF), D_MODEL ** -0.5),
        "w_ffn2_up": nrm(ks[19], (L, D_MODEL, D_FF), D_MODEL ** -0.5),
        "w_ffn2_down": nrm(ks[20], (L, D_FF, D_MODEL), D_FF ** -0.5),
        "g_ple": gain(ks[21], (L, D_MODEL)),
        "w_ple_gate": nrm(ks[22], (L, D_MODEL, D_MODEL), D_MODEL ** -0.5),
        "w_ple_proj": nrm(ks[23], (L, PLE_DIM, D_MODEL), PLE_DIM ** -0.5),
    }


def reference(x_prompt, x_sample, p_prompt, p_sample, g_ffn1, w_ffn1_gate, w_ffn1_up, w_ffn1_down, g_mix, w_in,
              b_igate, b_fgate, g_qn, g_kn, rpb, g_mh, w_out, g_ffn2, w_ffn2_gate, w_ffn2_up, w_ffn2_down,
              g_ple, w_ple_gate, w_ple_proj):
    y_prompt = run_trunk(x_prompt, p_prompt, g_ffn1, w_ffn1_gate, w_ffn1_up, w_ffn1_down, g_mix, w_in,
                         b_igate, b_fgate, g_qn, g_kn, rpb, g_mh, w_out, g_ffn2, w_ffn2_gate, w_ffn2_up,
                         w_ffn2_down, g_ple, w_ple_gate, w_ple_proj)
    y_sample = run_trunk(x_sample, p_sample, g_ffn1, w_ffn1_gate, w_ffn1_up, w_ffn1_down, g_mix, w_in,
                         b_igate, b_fgate, g_qn, g_kn, rpb, g_mh, w_out, g_ffn2, w_ffn2_gate, w_ffn2_up,
                         w_ffn2_down, g_ple, w_ple_gate, w_ple_proj)
    return (y_prompt, y_sample)
```

```python
import functools

import numpy as np
import jax
import jax.numpy as jnp
from jax import lax
from jax.experimental import pallas as pl
from jax.experimental.pallas import tpu as pltpu

F32 = jnp.float32
BF16 = jnp.bfloat16

GRID_W = 64
WIN_H = 8
WIN_W = 16
H_A = 8
DH_A = 128
H_M = 4
DH_M = 256
W_A = H_A * DH_A
W_M = H_M * DH_M
EPS = 1e-6
N_GATES = 4 * H_M
U_MAIN = 3 * W_A + 4 * W_M

LANES = 128
V7X_VMEM_BYTES = 64 * 1024 * 1024

TM = 512
TF = 512
TN_IN = 512
ATT_RB = 4
ATT_WIN = 12
ML_CHUNK = 256
NEG_BIG = -1e30


def _vmem_limit(block_bytes, scratch_bytes):
    est = 2 * block_bytes + scratch_bytes + 12 * 1024 * 1024
    return int(min(est, V7X_VMEM_BYTES - 6 * 1024 * 1024))


def _nbytes(shape, dtype):
    return int(np.prod(shape)) * jnp.dtype(dtype).itemsize


def _rms_rows(x, g):
    ms = jnp.mean(x * x, axis=-1, keepdims=True)
    return x * lax.rsqrt(ms + EPS) * g


def _ffn_kernel(x_ref, g_ref, wg_ref, wu_ref, wd_ref, o_ref, hn_ref, acc_ref, *, nf):
    j = pl.program_id(1)

    @pl.when(j == 0)
    def _():
        hn_ref[...] = _rms_rows(x_ref[...], g_ref[...]).astype(BF16)
        acc_ref[...] = jnp.zeros_like(acc_ref)

    hn = hn_ref[...]
    a = jnp.dot(hn, wg_ref[...], preferred_element_type=F32)
    b = jnp.dot(hn, wu_ref[...], preferred_element_type=F32)
    h = (a * jax.nn.sigmoid(a) * b).astype(BF16)
    acc_ref[...] += jnp.dot(h, wd_ref[...], preferred_element_type=F32)

    @pl.when(j == nf - 1)
    def _():
        o_ref[...] = x_ref[...] + 0.5 * acc_ref[...]


def _ffn(x, g, wg, wu, wd):
    T, D = x.shape
    F = wg.shape[1]
    nf = F // TF
    blocks = (_nbytes((TM, D), F32) * 2 + _nbytes((D, TF), BF16) * 3)
    scratch = _nbytes((TM, D), BF16) + _nbytes((TM, D), F32)
    return pl.pallas_call(
        functools.partial(_ffn_kernel, nf=nf),
        grid=(T // TM, nf),
        in_specs=[
            pl.BlockSpec((TM, D), lambda i, j: (i, 0)),
            pl.BlockSpec((1, D), lambda i, j: (0, 0)),
            pl.BlockSpec((D, TF), lambda i, j: (0, j)),
            pl.BlockSpec((D, TF), lambda i, j: (0, j)),
            pl.BlockSpec((TF, D), lambda i, j: (j, 0)),
        ],
        out_specs=pl.BlockSpec((TM, D), lambda i, j: (i, 0)),
        out_shape=jax.ShapeDtypeStruct((T, D), F32),
        scratch_shapes=[pltpu.VMEM((TM, D), BF16), pltpu.VMEM((TM, D), F32)],
        compiler_params=pltpu.CompilerParams(
            dimension_semantics=("arbitrary", "arbitrary"),
            vmem_limit_bytes=_vmem_limit(blocks, scratch)),
        name="ffn",
    )(x, g.reshape(1, D), wg, wu, wd)


def _inproj_kernel(x_ref, g_ref, w_ref, wgate_ref, gq_ref, gk_ref, u_ref, gates_ref, hn_ref):
    j = pl.program_id(1)

    @pl.when(j == 0)
    def _():
        hn = _rms_rows(x_ref[...], g_ref[...]).astype(BF16)
        hn_ref[...] = hn
        gates_ref[...] = jnp.dot(hn, wgate_ref[...], preferred_element_type=F32)

    acc = jnp.dot(hn_ref[...], w_ref[...], preferred_element_type=F32)
    n_qk_tiles = 2 * W_A // TN_IN

    @pl.when(j < n_qk_tiles)
    def _():
        is_q = j < W_A // TN_IN
        gain = jnp.where(is_q, gq_ref[...] * (DH_A ** -0.5), gk_ref[...])
        for hh in range(TN_IN // DH_A):
            sl = slice(hh * DH_A, (hh + 1) * DH_A)
            u_ref[:, sl] = _rms_rows(acc[:, sl], gain).astype(BF16)

    @pl.when(j >= n_qk_tiles)
    def _():
        u_ref[...] = acc.astype(BF16)


def _inproj(x, g, w_main, w_gate, g_qn, g_kn):
    T, D = x.shape
    nj = U_MAIN // TN_IN
    blocks = (_nbytes((TM, D), F32) + _nbytes((D, TN_IN), BF16) + _nbytes((D, LANES), BF16)
              + _nbytes((TM, TN_IN), BF16) + _nbytes((TM, LANES), F32))
    scratch = _nbytes((TM, D), BF16)
    return pl.pallas_call(
        _inproj_kernel,
        grid=(T // TM, nj),
        in_specs=[
            pl.BlockSpec((TM, D), lambda i, j: (i, 0)),
            pl.BlockSpec((1, D), lambda i, j: (0, 0)),
            pl.BlockSpec((D, TN_IN), lambda i, j: (0, j)),
            pl.BlockSpec((D, LANES), lambda i, j: (0, 0)),
            pl.BlockSpec((1, DH_A), lambda i, j: (0, 0)),
            pl.BlockSpec((1, DH_A), lambda i, j: (0, 0)),
        ],
        out_specs=[
            pl.BlockSpec((TM, TN_IN), lambda i, j: (i, j)),
            pl.BlockSpec((TM, LANES), lambda i, j: (i, 0)),
        ],
        out_shape=[
            jax.ShapeDtypeStruct((T, U_MAIN), BF16),
            jax.ShapeDtypeStruct((T, LANES), F32),
        ],
        scratch_shapes=[pltpu.VMEM((TM, D), BF16)],
        compiler_params=pltpu.CompilerParams(
            dimension_semantics=("arbitrary", "arbitrary"),
            vmem_limit_bytes=_vmem_limit(blocks, scratch)),
        name="inproj",
    )(x, g.reshape(1, D), w_main, w_gate, g_qn.reshape(1, DH_A), g_kn.reshape(1, DH_A))


def _attn_bias_table(rpb):
    a = np.arange(ATT_RB)[:, None]
    ik = np.arange(ATT_WIN)[None, :]
    kh = WIN_H
    valid0 = ik < kh
    dr0 = ik - a + (WIN_H - 1)
    valid1 = (ik >= a) & (ik < a + kh)
    dr1 = ik - a + (WIN_H - 1) - kh // 2
    valid2 = (ik >= ATT_WIN - kh) & (ik < ATT_WIN)
    dr2 = (ik - ATT_WIN) - (a - ATT_RB) + (WIN_H - 1)
    valid = np.stack([valid0 & (a >= 0), valid1, valid2 & (a >= 0)])
    dr = np.clip(np.stack([dr0, dr1, dr2]), 0, 2 * WIN_H - 2)

    c = np.arange(GRID_W)
    cs = np.clip(c - WIN_W // 2, 0, GRID_W - WIN_W)
    colmask = (c[None, :] >= cs[:, None]) & (c[None, :] < cs[:, None] + WIN_W)
    dc = np.clip(c[None, :] - c[:, None], -(WIN_W - 1), WIN_W - 1) + (WIN_W - 1)

    dr_i = dr[:, :, None, :, None]
    dc_i = dc[None, None, :, None, :]
    ok = valid[:, :, None, :, None] & colmask[None, None, :, None, :]
    bias = rpb.astype(F32)[:, dr_i, dc_i]
    bias = jnp.where(ok[None], bias, NEG_BIG)
    bias = jnp.transpose(bias, (1, 0, 2, 3, 4, 5))
    return bias.reshape(3, H_A, ATT_RB * GRID_W, ATT_WIN * GRID_W)


def _attn_kernel(q_ref, k_ref, v_ref, bias_ref, o_ref, *, n_rows):
    i = pl.program_id(2)
    nb = n_rows // ATT_RB
    ws = jnp.clip(i * ATT_RB - WIN_H // 2, 0, n_rows - ATT_WIN)
    start = pl.multiple_of(ws * GRID_W, GRID_W)
    cls = jnp.where(i == 0, 0, jnp.where(i == nb - 1, 2, 1))

    q = q_ref[0]
    kw = k_ref[0, pl.ds(start, ATT_WIN * GRID_W), :]
    vw = v_ref[0, pl.ds(start, ATT_WIN * GRID_W), :]
    s = lax.dot_general(q, kw, (((1,), (1,)), ((), ())), preferred_element_type=F32)
    s = s + bias_ref[cls, 0]
    m = jnp.max(s, axis=-1, keepdims=True)
    p = jnp.exp(s - m)
    l = jnp.sum(p, axis=-1, keepdims=True)
    o = jnp.dot(p.astype(BF16), vw, preferred_element_type=F32)
    o_ref[0] = (o / l).astype(BF16)


def _attention(u, bias, B, S):
    R = S // GRID_W
    assert R % ATT_RB == 0 and R >= ATT_WIN and R // ATT_RB >= 3
    tq = ATT_RB * GRID_W
    tk = ATT_WIN * GRID_W
    blocks = (_nbytes((tq, DH_A), BF16) * 2 + _nbytes((S, DH_A), BF16) * 2
              + _nbytes((3, tq, tk), F32))
    return pl.pallas_call(
        functools.partial(_attn_kernel, n_rows=R),
        grid=(H_A, B, R // ATT_RB),
        in_specs=[
            pl.BlockSpec((1, tq, DH_A), lambda h, b, i: (b, i, h)),
            pl.BlockSpec((1, S, DH_A), lambda h, b, i: (b, 0, H_A + h)),
            pl.BlockSpec((1, S, DH_A), lambda h, b, i: (b, 0, 2 * H_A + h)),
            pl.BlockSpec((3, 1, tq, tk), lambda h, b, i: (0, h, 0, 0)),
        ],
        out_specs=pl.BlockSpec((1, tq, DH_A), lambda h, b, i: (b, i, h)),
        out_shape=jax.ShapeDtypeStruct((B, S, W_A), BF16),
        compiler_params=pltpu.CompilerParams(
            dimension_semantics=("arbitrary", "arbitrary", "arbitrary"),
            vmem_limit_bytes=_vmem_limit(blocks, 0)),
        name="attention",
    )(u, u, u, bias)


def _mlstm_kernel(*refs, reverse, n_chunks):
    if reverse:
        (bi_ref, bf_ref, q_ref, k_ref, v_ref, gi_ref, gf_ref, hf_ref, og_ref, gmh_ref,
         o_ref, ct_ref, n_ref, m_ref, b_ref, e_ref) = refs
    else:
        (bi_ref, bf_ref, q_ref, k_ref, v_ref, gi_ref, gf_ref,
         o_ref, ct_ref, n_ref, m_ref, b_ref, e_ref) = refs
    L = ML_CHUNK
    h = pl.program_id(1)
    c = pl.program_id(2)
    d = 1 if reverse else 0

    row = lax.broadcasted_iota(jnp.int32, (L, L), 0)
    col = lax.broadcasted_iota(jnp.int32, (L, L), 1)
    tri = (col >= row) if reverse else (col <= row)

    @pl.when(c == 0)
    def _():
        ct_ref[...] = jnp.zeros_like(ct_ref)
        n_ref[...] = jnp.zeros_like(n_ref)
        m_ref[...] = jnp.zeros_like(m_ref)
        ig = gi_ref[0, 0] + bi_ref[d, h]
        lf = jax.nn.log_sigmoid(gf_ref[0, 0] + bf_ref[d, h])
        cum = jnp.where((row >= col) if reverse else (row <= col), 1.0, 0.0).astype(F32)
        b = jnp.dot(lf, cum, preferred_element_type=F32, precision=lax.Precision.HIGHEST)
        b_ref[...] = b
        e_ref[...] = ig - b

    cc = (n_chunks - 1 - c) if reverse else c
    b_row = b_ref[pl.ds(cc, 1), :]
    e_row = e_ref[pl.ds(cc, 1), :]
    eye = row == col
    b_col = jnp.sum(jnp.where(eye, b_row, 0.0), axis=1, keepdims=True)
    e_col = jnp.sum(jnp.where(eye, e_row, 0.0), axis=1, keepdims=True)
    a_tot = b_row[:, 0:1] if reverse else b_row[:, L - 1:L]
    m_prev = m_ref[...]

    q = q_ref[0]
    k = k_ref[0] * jnp.asarray(DH_M ** -0.5, BF16)
    v = v_ref[0]

    dmat = jnp.where(tri, b_col + e_row, -jnp.inf)
    inter = b_col + m_prev
    m_t = jnp.maximum(inter, jnp.max(dmat, axis=1, keepdims=True))
    dexp = jnp.exp(dmat - m_t)
    s = lax.dot_general(q, k, (((1,), (1,)), ((), ())), preferred_element_type=F32)
    sqk = s * dexp
    sc = jnp.exp(inter - m_t)
    ct = ct_ref[...]
    qf = q.astype(F32)
    num = sc * jnp.dot(q, ct.astype(BF16), preferred_element_type=F32) + jnp.dot(
        sqk.astype(BF16), v, preferred_element_type=F32)
    den = sc * jnp.sum(qf * n_ref[...], axis=1, keepdims=True) + jnp.sum(sqk, axis=1, keepdims=True)
    hdir = num / jnp.maximum(jnp.abs(den), jnp.exp(-m_t))

    g_col = a_tot + e_col
    m_loc = jnp.max(g_col, axis=0, keepdims=True)
    m_new = jnp.maximum(a_tot + m_prev, m_loc)
    s_prev = jnp.exp(a_tot + m_prev - m_new)
    w_col = jnp.exp(g_col - m_new)
    vw = (v.astype(F32) * w_col).astype(BF16)
    ct_ref[...] = s_prev * ct + lax.dot_general(
        k, vw, (((0,), (0,)), ((), ())), preferred_element_type=F32)
    n_ref[...] = s_prev * n_ref[...] + jnp.sum(k.astype(F32) * w_col, axis=0, keepdims=True)
    m_ref[...] = m_new

    if reverse:
        hm = hf_ref[0] + hdir
        hm = _rms_rows(hm, gmh_ref[...])
        o_ref[0] = (jax.nn.sigmoid(og_ref[0].astype(F32)) * hm).astype(BF16)
    else:
        o_ref[0] = hdir


def _mlstm_sweep(u, gates_t, b_igate, b_fgate, B, S, *, reverse, hf=None, g_mh=None):
    L = ML_CHUNK
    N = S // L
    d = 1 if reverse else 0
    qoff = 3 * W_A // DH_M
    cidx = (lambda c: N - 1 - c) if reverse else (lambda c: c)
    smem = pl.BlockSpec(memory_space=pltpu.SMEM)
    in_specs = [
        smem, smem,
        pl.BlockSpec((1, L, DH_M), lambda b, h, c: (b, cidx(c), qoff + h)),
        pl.BlockSpec((1, L, DH_M), lambda b, h, c: (b, cidx(c), qoff + H_M + h)),
        pl.BlockSpec((1, L, DH_M), lambda b, h, c: (b, cidx(c), qoff + 2 * H_M + h)),
        pl.BlockSpec((1, 1, N, L), lambda b, h, c: (b, d * H_M + h, 0, 0)),
        pl.BlockSpec((1, 1, N, L), lambda b, h, c: (b, 2 * H_M + d * H_M + h, 0, 0)),
    ]
    args = [b_igate, b_fgate, u, u, u, gates_t, gates_t]
    if reverse:
        in_specs += [
            pl.BlockSpec((1, L, DH_M), lambda b, h, c: (b, cidx(c), h)),
            pl.BlockSpec((1, L, DH_M), lambda b, h, c: (b, cidx(c), qoff + 3 * H_M + h)),
            pl.BlockSpec((1, DH_M), lambda b, h, c: (0, h)),
        ]
        args += [hf, u, g_mh.reshape(1, W_M)]
        out_dtype = BF16
    else:
        out_dtype = F32
    blocks = (_nbytes((L, DH_M), BF16) * 4 + _nbytes((N, L), F32) * 2 + _nbytes((L, DH_M), F32) * 2)
    scratch = _nbytes((DH_M, DH_M), F32) + 2 * _nbytes((N, L), F32) + 2 * 4096
    return pl.pallas_call(
        functools.partial(_mlstm_kernel, reverse=reverse, n_chunks=N),
        grid=(B, H_M, N),
        in_specs=in_specs,
        out_specs=pl.BlockSpec((1, L, DH_M), lambda b, h, c: (b, cidx(c), h)),
        out_shape=jax.ShapeDtypeStruct((B, S, W_M), out_dtype),
        scratch_shapes=[
            pltpu.VMEM((DH_M, DH_M), F32),
            pltpu.VMEM((1, DH_M), F32),
            pltpu.VMEM((1, 1), F32),
            pltpu.VMEM((N, L), F32),
            pltpu.VMEM((N, L), F32),
        ],
        compiler_params=pltpu.CompilerParams(
            dimension_semantics=("arbitrary", "arbitrary", "arbitrary"),
            vmem_limit_bytes=_vmem_limit(blocks, scratch)),
        name="mlstm_bwd" if reverse else "mlstm_fwd",
    )(*args)


def _outproj_kernel(x_ref, ya_ref, ym_ref, wa_ref, wm_ref, o_ref):
    o_ref[...] = (x_ref[...]
                  + jnp.dot(ya_ref[...], wa_ref[...], preferred_element_type=F32)
                  + jnp.dot(ym_ref[...], wm_ref[...], preferred_element_type=F32))


def _outproj(x, ya, ym, w_a, w_m):
    T, D = x.shape
    blocks = (_nbytes((TM, D), F32) * 2 + _nbytes((TM, W_A), BF16) * 2 + _nbytes((W_A, D), BF16) * 2)
    return pl.pallas_call(
        _outproj_kernel,
        grid=(T // TM,),
        in_specs=[
            pl.BlockSpec((TM, D), lambda i: (i, 0)),
            pl.BlockSpec((TM, W_A), lambda i: (i, 0)),
            pl.BlockSpec((TM, W_M), lambda i: (i, 0)),
            pl.BlockSpec((W_A, D), lambda i: (0, 0)),
            pl.BlockSpec((W_M, D), lambda i: (0, 0)),
        ],
        out_specs=pl.BlockSpec((TM, D), lambda i: (i, 0)),
        out_shape=jax.ShapeDtypeStruct((T, D), F32),
        compiler_params=pltpu.CompilerParams(
            dimension_semantics=("arbitrary",),
            vmem_limit_bytes=_vmem_limit(blocks, 0)),
        name="outproj",
    )(x, ya, ym, w_a, w_m)


def _ple_kernel(x_ref, g_ref, pe_ref, wg_ref, wp_ref, o_ref):
    x = x_ref[...]
    hn = _rms_rows(x, g_ref[...]).astype(BF16)
    gate = jax.nn.sigmoid(jnp.dot(hn, wg_ref[...], preferred_element_type=F32))
    proj = jnp.dot(pe_ref[...].astype(BF16), wp_ref[...], preferred_element_type=F32)
    o_ref[...] = x + gate * proj


def _ple(x, g, pe, w_gate, w_proj):
    T, D = x.shape
    P = pe.shape[1]
    blocks = (_nbytes((TM, D), F32) * 2 + _nbytes((TM, P), F32) + _nbytes((D, D), BF16)
              + _nbytes((P, D), BF16))
    return pl.pallas_call(
        _ple_kernel,
        grid=(T // TM,),
        in_specs=[
            pl.BlockSpec((TM, D), lambda i: (i, 0)),
            pl.BlockSpec((1, D), lambda i: (0, 0)),
            pl.BlockSpec((TM, P), lambda i: (i, 0)),
            pl.BlockSpec((D, D), lambda i: (0, 0)),
            pl.BlockSpec((P, D), lambda i: (0, 0)),
        ],
        out_specs=pl.BlockSpec((TM, D), lambda i: (i, 0)),
        out_shape=jax.ShapeDtypeStruct((T, D), F32),
        compiler_params=pltpu.CompilerParams(
            dimension_semantics=("arbitrary",),
            vmem_limit_bytes=_vmem_limit(blocks, 0)),
        name="ple",
    )(x, g.reshape(1, D), pe, w_gate, w_proj)


def _layer(x, pe, w):
    B, S, D = x.shape
    T = B * S
    x2 = x.reshape(T, D)
    x2 = _ffn(x2, w["g_ffn1"], w["w1g"], w["w1u"], w["w1d"])

    u, gates = _inproj(x2, w["g_mix"], w["w_in_main"], w["w_in_gate"], w["g_qn"], w["g_kn"])
    u = u.reshape(B, S, U_MAIN)
    ya = _attention(u, w["attn_bias"], B, S)

    n_chunks = S // ML_CHUNK
    gates_t = jnp.transpose(gates[:, :N_GATES].reshape(B, S, N_GATES), (0, 2, 1))
    gates_t = gates_t.reshape(B, N_GATES, n_chunks, ML_CHUNK)
    hf = _mlstm_sweep(u, gates_t, w["b_igate"], w["b_fgate"], B, S, reverse=False)
    ym = _mlstm_sweep(u, gates_t, w["b_igate"], w["b_fgate"], B, S, reverse=True,
                      hf=hf, g_mh=w["g_mh"])

    x2 = _outproj(x2, ya.reshape(T, W_A), ym.reshape(T, W_M), w["w_out_a"], w["w_out_m"])
    x2 = _ffn(x2, w["g_ffn2"], w["w2g"], w["w2u"], w["w2d"])
    x2 = _ple(x2, w["g_ple"], pe.reshape(T, -1), w["w_ple_gate"], w["w_ple_proj"])
    return x2.reshape(B, S, D)


def kernel(x_prompt, x_sample, p_prompt, p_sample, g_ffn1, w_ffn1_gate, w_ffn1_up, w_ffn1_down, g_mix, w_in, b_igate, b_fgate, g_qn, g_kn, rpb, g_mh, w_out, g_ffn2, w_ffn2_gate, w_ffn2_up, w_ffn2_down, g_ple, w_ple_gate, w_ple_proj):
    depth = g_ffn1.shape[0]
    xs = [x_prompt, x_sample]
    ps = [p_prompt, p_sample]
    for i in range(depth):
        w_in_i = w_in[i]
        gate_cols = jnp.pad(w_in_i[:, U_MAIN:], ((0, 0), (0, LANES - N_GATES)))
        w = {
            "g_ffn1": g_ffn1[i], "w1g": w_ffn1_gate[i].astype(BF16), "w1u": w_ffn1_up[i].astype(BF16),
            "w1d": w_ffn1_down[i].astype(BF16),
            "g_mix": g_mix[i], "w_in_main": w_in_i[:, :U_MAIN].astype(BF16),
            "w_in_gate": gate_cols.astype(BF16),
            "b_igate": b_igate[i], "b_fgate": b_fgate[i], "g_qn": g_qn[i], "g_kn": g_kn[i],
            "attn_bias": _attn_bias_table(rpb[i]), "g_mh": g_mh[i],
            "w_out_a": w_out[i, :W_A].astype(BF16), "w_out_m": w_out[i, W_A:].astype(BF16),
            "g_ffn2": g_ffn2[i], "w2g": w_ffn2_gate[i].astype(BF16), "w2u": w_ffn2_up[i].astype(BF16),
            "w2d": w_ffn2_down[i].astype(BF16),
            "g_ple": g_ple[i], "w_ple_gate": w_ple_gate[i].astype(BF16),
            "w_ple_proj": w_ple_proj[i].astype(BF16),
        }
        xs = [_layer(x, p[i], w) for x, p in zip(xs, ps)]
    return (xs[0], xs[1])
```

```python
import functools

import numpy as np
import jax
import jax.numpy as jnp
from jax import lax
from jax.experimental import pallas as pl
from jax.experimental.pallas import tpu as pltpu

F32 = jnp.float32
BF16 = jnp.bfloat16

GRID_W = 64
WIN_H = 8
WIN_W = 16
H_A = 8
DH_A = 128
H_M = 4
DH_M = 256
W_A = H_A * DH_A
W_M = H_M * DH_M
EPS = 1e-6
N_GATES = 4 * H_M
U_MAIN = 3 * W_A + 4 * W_M

LANES = 128
V7X_VMEM_BYTES = 64 * 1024 * 1024

TM = 512
TM_IN = 1024
TF = 512
TN_IN = 512
ATT_RB = 4
ATT_WIN = 12
ATT_HEADS = 2
ML_CHUNK = 256
NEG_BIG = -1e30


def _vmem_limit(block_bytes, scratch_bytes):
    est = 2 * block_bytes + scratch_bytes + 12 * 1024 * 1024
    return int(min(est, V7X_VMEM_BYTES - 6 * 1024 * 1024))


def _nbytes(shape, dtype):
    return int(np.prod(shape)) * jnp.dtype(dtype).itemsize


def _rms_rows(x, g):
    ms = jnp.mean(x * x, axis=-1, keepdims=True)
    return x * lax.rsqrt(ms + EPS) * g


def _ffn_kernel(x_ref, g_ref, wg_ref, wu_ref, wd_ref, o_ref, hn_ref):
    j = pl.program_id(1)

    @pl.when(j == 0)
    def _():
        x = x_ref[...]
        hn_ref[...] = _rms_rows(x, g_ref[...]).astype(BF16)
        o_ref[...] = x

    hn = hn_ref[...]
    a = jnp.dot(hn, wg_ref[...], preferred_element_type=F32)
    b = jnp.dot(hn, wu_ref[...], preferred_element_type=F32)
    h = (a * jax.nn.sigmoid(a) * (0.5 * b)).astype(BF16)
    o_ref[...] += jnp.dot(h, wd_ref[...], preferred_element_type=F32)


def _ffn(x, g, wg, wu, wd):
    T, D = x.shape
    F = wg.shape[1]
    blocks = (_nbytes((TM, D), F32) * 2 + _nbytes((D, TF), BF16) * 3)
    scratch = _nbytes((TM, D), BF16)
    return pl.pallas_call(
        _ffn_kernel,
        grid=(T // TM, F // TF),
        in_specs=[
            pl.BlockSpec((TM, D), lambda i, j: (i, 0)),
            pl.BlockSpec((1, D), lambda i, j: (0, 0)),
            pl.BlockSpec((D, TF), lambda i, j: (0, j)),
            pl.BlockSpec((D, TF), lambda i, j: (0, j)),
            pl.BlockSpec((TF, D), lambda i, j: (j, 0)),
        ],
        out_specs=pl.BlockSpec((TM, D), lambda i, j: (i, 0)),
        out_shape=jax.ShapeDtypeStruct((T, D), F32),
        scratch_shapes=[pltpu.VMEM((TM, D), BF16)],
        compiler_params=pltpu.CompilerParams(
            dimension_semantics=("arbitrary", "arbitrary"),
            vmem_limit_bytes=_vmem_limit(blocks, scratch)),
        name="ffn",
    )(x, g.reshape(1, D), wg, wu, wd)


def _inproj_kernel(x_ref, g_ref, w_ref, wgate_ref, gq_ref, gk_ref, u_ref, gates_ref, hn_ref):
    j = pl.program_id(1)

    @pl.when(j == 0)
    def _():
        hn = _rms_rows(x_ref[...], g_ref[...]).astype(BF16)
        hn_ref[...] = hn
        gates_ref[...] = jnp.dot(hn, wgate_ref[...], preferred_element_type=F32)

    acc = jnp.dot(hn_ref[...], w_ref[...], preferred_element_type=F32)
    u_ref[...] = acc.astype(BF16)
    n_qk_tiles = 2 * W_A // TN_IN

    @pl.when(j < n_qk_tiles)
    def _():
        is_q = j < W_A // TN_IN
        gain = jnp.where(is_q, gq_ref[...] * (DH_A ** -0.5), gk_ref[...])
        for hh in range(TN_IN // DH_A):
            sl = slice(hh * DH_A, (hh + 1) * DH_A)
            u_ref[:, sl] = _rms_rows(acc[:, sl], gain).astype(BF16)


def _inproj(x, g, w_main, w_gate, g_qn, g_kn):
    T, D = x.shape
    tm = TM_IN
    nj = U_MAIN // TN_IN
    blocks = (_nbytes((tm, D), F32) + _nbytes((D, TN_IN), BF16) + _nbytes((D, LANES), BF16)
              + _nbytes((tm, TN_IN), BF16) + _nbytes((tm, LANES), F32))
    scratch = _nbytes((tm, D), BF16)
    return pl.pallas_call(
        _inproj_kernel,
        grid=(T // tm, nj),
        in_specs=[
            pl.BlockSpec((tm, D), lambda i, j: (i, 0)),
            pl.BlockSpec((1, D), lambda i, j: (0, 0)),
            pl.BlockSpec((D, TN_IN), lambda i, j: (0, j)),
            pl.BlockSpec((D, LANES), lambda i, j: (0, 0)),
            pl.BlockSpec((1, DH_A), lambda i, j: (0, 0)),
            pl.BlockSpec((1, DH_A), lambda i, j: (0, 0)),
        ],
        out_specs=[
            pl.BlockSpec((tm, TN_IN), lambda i, j: (i, j)),
            pl.BlockSpec((tm, LANES), lambda i, j: (i, 0)),
        ],
        out_shape=[
            jax.ShapeDtypeStruct((T, U_MAIN), BF16),
            jax.ShapeDtypeStruct((T, LANES), F32),
        ],
        scratch_shapes=[pltpu.VMEM((tm, D), BF16)],
        compiler_params=pltpu.CompilerParams(
            dimension_semantics=("arbitrary", "arbitrary"),
            vmem_limit_bytes=_vmem_limit(blocks, scratch)),
        name="inproj",
    )(x, g.reshape(1, D), w_main, w_gate, g_qn.reshape(1, DH_A), g_kn.reshape(1, DH_A))


def _attn_bias_table(rpb):
    a = np.arange(ATT_RB)[:, None]
    ik = np.arange(ATT_WIN)[None, :]
    kh = WIN_H
    valid0 = ik < kh
    dr0 = ik - a + (WIN_H - 1)
    valid1 = (ik >= a) & (ik < a + kh)
    dr1 = ik - a + (WIN_H - 1) - kh // 2
    valid2 = (ik >= ATT_WIN - kh) & (ik < ATT_WIN)
    dr2 = (ik - ATT_WIN) - (a - ATT_RB) + (WIN_H - 1)
    valid = np.stack([np.broadcast_to(valid0, dr0.shape), valid1,
                      np.broadcast_to(valid2, dr0.shape)])
    dr = np.stack([dr0, dr1, dr2])

    c = np.arange(GRID_W)
    cs = np.clip(c - WIN_W // 2, 0, GRID_W - WIN_W)
    colmask = (c[None, :] >= cs[:, None]) & (c[None, :] < cs[:, None] + WIN_W)
    dc = np.clip(c[None, :] - c[:, None], -(WIN_W - 1), WIN_W - 1) + (WIN_W - 1)

    onehot = (dc[None] == np.arange(2 * WIN_W - 1)[:, None, None]).astype(np.float32)
    tiles = jnp.einsum("hrc,cqk->hrqk", rpb.astype(F32), jnp.asarray(onehot),
                       precision=lax.Precision.HIGHEST)
    tiles = jnp.where(jnp.asarray(colmask), tiles, NEG_BIG)
    masked = jnp.full((H_A, GRID_W, GRID_W), NEG_BIG, F32)
    classes = []
    for cls in range(3):
        rows = []
        for qa in range(ATT_RB):
            rows.append(jnp.concatenate(
                [tiles[:, int(dr[cls, qa, kk])] if valid[cls, qa, kk] else masked
                 for kk in range(ATT_WIN)], axis=-1))
        classes.append(jnp.concatenate(rows, axis=1))
    return jnp.stack(classes)


def _attn_kernel(q_ref, k_ref, v_ref, bias_ref, o_ref, *, n_rows):
    i = pl.program_id(2)
    nb = n_rows // ATT_RB
    ws = jnp.clip(i * ATT_RB - WIN_H // 2, 0, n_rows - ATT_WIN)
    start = pl.multiple_of(ws * GRID_W, GRID_W)
    cls = jnp.where(i == 0, 0, jnp.where(i == nb - 1, 2, 1))

    for hh in range(ATT_HEADS):
        sl = slice(hh * DH_A, (hh + 1) * DH_A)
        q = q_ref[0, :, sl]
        kw = k_ref[0, pl.ds(start, ATT_WIN * GRID_W), sl]
        vw = v_ref[0, pl.ds(start, ATT_WIN * GRID_W), sl]
        s = lax.dot_general(q, kw, (((1,), (1,)), ((), ())), preferred_element_type=F32)
        s = s + bias_ref[cls, hh]
        m = jnp.max(s, axis=-1, keepdims=True)
        p = jnp.exp(s - m)
        l = jnp.sum(p, axis=-1, keepdims=True)
        o = jnp.dot(p.astype(BF16), vw, preferred_element_type=F32)
        o_ref[0, :, sl] = (o / l).astype(BF16)


def _attention(u, bias, B, S):
    R = S // GRID_W
    assert R % ATT_RB == 0 and R >= ATT_WIN and R // ATT_RB >= 3
    tq = ATT_RB * GRID_W
    tk = ATT_WIN * GRID_W
    hw = ATT_HEADS * DH_A
    ng = H_A // ATT_HEADS
    blocks = (_nbytes((tq, hw), BF16) * 2 + _nbytes((S, hw), BF16) * 2
              + _nbytes((3, ATT_HEADS, tq, tk), F32))
    return pl.pallas_call(
        functools.partial(_attn_kernel, n_rows=R),
        grid=(ng, B, R // ATT_RB),
        in_specs=[
            pl.BlockSpec((1, tq, hw), lambda h, b, i: (b, i, h)),
            pl.BlockSpec((1, S, hw), lambda h, b, i: (b, 0, ng + h)),
            pl.BlockSpec((1, S, hw), lambda h, b, i: (b, 0, 2 * ng + h)),
            pl.BlockSpec((3, ATT_HEADS, tq, tk), lambda h, b, i: (0, h, 0, 0)),
        ],
        out_specs=pl.BlockSpec((1, tq, hw), lambda h, b, i: (b, i, h)),
        out_shape=jax.ShapeDtypeStruct((B, S, W_A), BF16),
        compiler_params=pltpu.CompilerParams(
            dimension_semantics=("arbitrary", "arbitrary", "arbitrary"),
            vmem_limit_bytes=_vmem_limit(blocks, 0)),
        name="attention",
    )(u, u, u, bias)


def _mlstm_kernel(*refs, reverse, n_chunks):
    if reverse:
        (bi_ref, bf_ref, q_ref, k_ref, v_ref, gi_ref, gf_ref, hf_ref, og_ref, gmh_ref,
         o_ref, ct_ref, n_ref, m_ref, b_ref, e_ref) = refs
    else:
        (bi_ref, bf_ref, q_ref, k_ref, v_ref, gi_ref, gf_ref,
         o_ref, ct_ref, n_ref, m_ref, b_ref, e_ref) = refs
    L = ML_CHUNK
    c = pl.program_id(1)
    d = 1 if reverse else 0

    row = lax.broadcasted_iota(jnp.int32, (L, L), 0)
    col = lax.broadcasted_iota(jnp.int32, (L, L), 1)
    tri = (col >= row) if reverse else (col <= row)
    eye = row == col

    @pl.when(c == 0)
    def _():
        ct_ref[...] = jnp.zeros_like(ct_ref)
        n_ref[...] = jnp.zeros_like(n_ref)
        m_ref[...] = jnp.zeros_like(m_ref)
        cum = jnp.where((row >= col) if reverse else (row <= col), 1.0, 0.0).astype(F32)
        for hh in range(H_M):
            ig = gi_ref[0, hh] + bi_ref[d, hh]
            lf = jax.nn.log_sigmoid(gf_ref[0, hh] + bf_ref[d, hh])
            b = jnp.dot(lf, cum, preferred_element_type=F32, precision=lax.Precision.HIGHEST)
            b_ref[hh] = b
            e_ref[hh] = ig - b

    cc = (n_chunks - 1 - c) if reverse else c

    for hh in range(H_M):
        sl = slice(hh * DH_M, (hh + 1) * DH_M)
        b_row = b_ref[hh, pl.ds(cc, 1), :]
        e_row = e_ref[hh, pl.ds(cc, 1), :]
        b_col = jnp.sum(jnp.where(eye, b_row, 0.0), axis=1, keepdims=True)
        e_col = jnp.sum(jnp.where(eye, e_row, 0.0), axis=1, keepdims=True)
        a_tot = b_row[:, 0:1] if reverse else b_row[:, L - 1:L]
        m_prev = m_ref[hh]

        q = q_ref[0, :, sl]
        k = k_ref[0, :, sl] * jnp.asarray(DH_M ** -0.5, BF16)
        v = v_ref[0, :, sl]

        dmat = jnp.where(tri, b_col + e_row, -jnp.inf)
        inter = b_col + m_prev
        m_t = jnp.maximum(inter, jnp.max(dmat, axis=1, keepdims=True))
        dexp = jnp.exp(dmat - m_t)
        s = lax.dot_general(q, k, (((1,), (1,)), ((), ())), preferred_element_type=F32)
        sqk = s * dexp
        sc = jnp.exp(inter - m_t)
        ct = ct_ref[hh]
        n_prev = n_ref[hh]
        num = sc * jnp.dot(q, ct.astype(BF16), preferred_element_type=F32) + jnp.dot(
            sqk.astype(BF16), v, preferred_element_type=F32)
        den = (sc * jnp.sum(q.astype(F32) * n_prev, axis=1, keepdims=True)
               + jnp.sum(sqk, axis=1, keepdims=True))
        hdir = num / jnp.maximum(jnp.abs(den), jnp.exp(-m_t))

        g_col = a_tot + e_col
        m_loc = jnp.max(g_col, axis=0, keepdims=True)
        m_new = jnp.maximum(a_tot + m_prev, m_loc)
        s_prev = jnp.exp(a_tot + m_prev - m_new)
        w_col = jnp.exp(g_col - m_new)
        vw = (v.astype(F32) * w_col).astype(BF16)
        ct_ref[hh] = s_prev * ct + lax.dot_general(
            k, vw, (((0,), (0,)), ((), ())), preferred_element_type=F32)
        n_ref[hh] = s_prev * n_prev + jnp.sum(k.astype(F32) * w_col, axis=0, keepdims=True)
        m_ref[hh] = m_new

        if reverse:
            hm = _rms_rows(hf_ref[0, :, sl] + hdir, gmh_ref[:, sl])
            o_ref[0, :, sl] = (jax.nn.sigmoid(og_ref[0, :, sl].astype(F32)) * hm).astype(BF16)
        else:
            o_ref[0, :, sl] = hdir


def _mlstm_sweep(u, gates_t, b_igate, b_fgate, B, S, *, reverse, hf=None, g_mh=None):
    L = ML_CHUNK
    N = S // L
    d = 1 if reverse else 0
    qoff = 3 * W_A // W_M
    cidx = (lambda c: N - 1 - c) if reverse else (lambda c: c)
    smem = pl.BlockSpec(memory_space=pltpu.SMEM)
    in_specs = [
        smem, smem,
        pl.BlockSpec((1, L, W_M), lambda b, c: (b, cidx(c), qoff)),
        pl.BlockSpec((1, L, W_M), lambda b, c: (b, cidx(c), qoff + 1)),
        pl.BlockSpec((1, L, W_M), lambda b, c: (b, cidx(c), qoff + 2)),
        pl.BlockSpec((1, H_M, N, L), lambda b, c: (b, d, 0, 0)),
        pl.BlockSpec((1, H_M, N, L), lambda b, c: (b, 2 + d, 0, 0)),
    ]
    args = [b_igate, b_fgate, u, u, u, gates_t, gates_t]
    if reverse:
        in_specs += [
            pl.BlockSpec((1, L, W_M), lambda b, c: (b, cidx(c), 0)),
            pl.BlockSpec((1, L, W_M), lambda b, c: (b, cidx(c), qoff + 3)),
            pl.BlockSpec((1, W_M), lambda b, c: (0, 0)),
        ]
        args += [hf, u, g_mh.reshape(1, W_M)]
        out_dtype = BF16
    else:
        out_dtype = F32
    blocks = (_nbytes((L, W_M), BF16) * 4 + _nbytes((H_M, N, L), F32) * 2 + _nbytes((L, W_M), F32) * 2)
    scratch = (H_M * _nbytes((DH_M, DH_M), F32) + 2 * _nbytes((H_M, N, L), F32)
               + 2 * H_M * 8 * LANES * 4)
    return pl.pallas_call(
        functools.partial(_mlstm_kernel, reverse=reverse, n_chunks=N),
        grid=(B, N),
        in_specs=in_specs,
        out_specs=pl.BlockSpec((1, L, W_M), lambda b, c: (b, cidx(c), 0)),
        out_shape=jax.ShapeDtypeStruct((B, S, W_M), out_dtype),
        scratch_shapes=[
            pltpu.VMEM((H_M, DH_M, DH_M), F32),
            pltpu.VMEM((H_M, 1, DH_M), F32),
            pltpu.VMEM((H_M, 1, 1), F32),
            pltpu.VMEM((H_M, N, L), F32),
            pltpu.VMEM((H_M, N, L), F32),
        ],
        compiler_params=pltpu.CompilerParams(
            dimension_semantics=("arbitrary", "arbitrary"),
            vmem_limit_bytes=_vmem_limit(blocks, scratch)),
        name="mlstm_bwd" if reverse else "mlstm_fwd",
    )(*args)


def _outproj_kernel(x_ref, ya_ref, ym_ref, wa_ref, wm_ref, o_ref):
    o_ref[...] = (x_ref[...]
                  + jnp.dot(ya_ref[...], wa_ref[...], preferred_element_type=F32)
                  + jnp.dot(ym_ref[...], wm_ref[...], preferred_element_type=F32))


def _outproj(x, ya, ym, w_a, w_m):
    T, D = x.shape
    blocks = (_nbytes((TM, D), F32) * 2 + _nbytes((TM, W_A), BF16) * 2 + _nbytes((W_A, D), BF16) * 2)
    return pl.pallas_call(
        _outproj_kernel,
        grid=(T // TM,),
        in_specs=[
            pl.BlockSpec((TM, D), lambda i: (i, 0)),
            pl.BlockSpec((TM, W_A), lambda i: (i, 0)),
            pl.BlockSpec((TM, W_M), lambda i: (i, 0)),
            pl.BlockSpec((W_A, D), lambda i: (0, 0)),
            pl.BlockSpec((W_M, D), lambda i: (0, 0)),
        ],
        out_specs=pl.BlockSpec((TM, D), lambda i: (i, 0)),
        out_shape=jax.ShapeDtypeStruct((T, D), F32),
        compiler_params=pltpu.CompilerParams(
            dimension_semantics=("arbitrary",),
            vmem_limit_bytes=_vmem_limit(blocks, 0)),
        name="outproj",
    )(x, ya, ym, w_a, w_m)


def _ple_kernel(x_ref, g_ref, pe_ref, wg_ref, wp_ref, o_ref):
    x = x_ref[...]
    hn = _rms_rows(x, g_ref[...]).astype(BF16)
    gate = jax.nn.sigmoid(jnp.dot(hn, wg_ref[...], preferred_element_type=F32))
    proj = jnp.dot(pe_ref[...].astype(BF16), wp_ref[...], preferred_element_type=F32)
    o_ref[...] = x + gate * proj


def _ple(x, g, pe, w_gate, w_proj):
    T, D = x.shape
    P = pe.shape[1]
    blocks = (_nbytes((TM, D), F32) * 2 + _nbytes((TM, P), F32) + _nbytes((D, D), BF16)
              + _nbytes((P, D), BF16))
    return pl.pallas_call(
        _ple_kernel,
        grid=(T // TM,),
        in_specs=[
            pl.BlockSpec((TM, D), lambda i: (i, 0)),
            pl.BlockSpec((1, D), lambda i: (0, 0)),
            pl.BlockSpec((TM, P), lambda i: (i, 0)),
            pl.BlockSpec((D, D), lambda i: (0, 0)),
            pl.BlockSpec((P, D), lambda i: (0, 0)),
        ],
        out_specs=pl.BlockSpec((TM, D), lambda i: (i, 0)),
        out_shape=jax.ShapeDtypeStruct((T, D), F32),
        compiler_params=pltpu.CompilerParams(
            dimension_semantics=("arbitrary",),
            vmem_limit_bytes=_vmem_limit(blocks, 0)),
        name="ple",
    )(x, g.reshape(1, D), pe, w_gate, w_proj)


def _layer(x, pe, w):
    B, S, D = x.shape
    T = B * S
    x2 = x.reshape(T, D)
    x2 = _ffn(x2, w["g_ffn1"], w["w1g"], w["w1u"], w["w1d"])

    u, gates = _inproj(x2, w["g_mix"], w["w_in_main"], w["w_in_gate"], w["g_qn"], w["g_kn"])
    u = u.reshape(B, S, U_MAIN)
    ya = _attention(u, w["attn_bias"], B, S)

    n_chunks = S // ML_CHUNK
    gates_t = jnp.transpose(gates[:, :N_GATES].reshape(B, S, N_GATES), (0, 2, 1))
    gates_t = gates_t.reshape(B, N_GATES, n_chunks, ML_CHUNK)
    hf = _mlstm_sweep(u, gates_t, w["b_igate"], w["b_fgate"], B, S, reverse=False)
    ym = _mlstm_sweep(u, gates_t, w["b_igate"], w["b_fgate"], B, S, reverse=True,
                      hf=hf, g_mh=w["g_mh"])

    x2 = _outproj(x2, ya.reshape(T, W_A), ym.reshape(T, W_M), w["w_out_a"], w["w_out_m"])
    x2 = _ffn(x2, w["g_ffn2"], w["w2g"], w["w2u"], w["w2d"])
    x2 = _ple(x2, w["g_ple"], pe.reshape(T, -1), w["w_ple_gate"], w["w_ple_proj"])
    return x2.reshape(B, S, D)


def kernel(x_prompt, x_sample, p_prompt, p_sample, g_ffn1, w_ffn1_gate, w_ffn1_up, w_ffn1_down, g_mix, w_in, b_igate, b_fgate, g_qn, g_kn, rpb, g_mh, w_out, g_ffn2, w_ffn2_gate, w_ffn2_up, w_ffn2_down, g_ple, w_ple_gate, w_ple_proj):
    depth = g_ffn1.shape[0]
    xs = [x_prompt, x_sample]
    ps = [p_prompt, p_sample]
    for i in range(depth):
        w_in_i = w_in[i]
        gate_cols = jnp.pad(w_in_i[:, U_MAIN:], ((0, 0), (0, LANES - N_GATES)))
        w = {
            "g_ffn1": g_ffn1[i], "w1g": w_ffn1_gate[i].astype(BF16), "w1u": w_ffn1_up[i].astype(BF16),
            "w1d": w_ffn1_down[i].astype(BF16),
            "g_mix": g_mix[i], "w_in_main": w_in_i[:, :U_MAIN].astype(BF16),
            "w_in_gate": gate_cols.astype(BF16),
            "b_igate": b_igate[i], "b_fgate": b_fgate[i], "g_qn": g_qn[i], "g_kn": g_kn[i],
            "attn_bias": _attn_bias_table(rpb[i]), "g_mh": g_mh[i],
            "w_out_a": w_out[i, :W_A].astype(BF16), "w_out_m": w_out[i, W_A:].astype(BF16),
            "g_ffn2": g_ffn2[i], "w2g": w_ffn2_gate[i].astype(BF16), "w2u": w_ffn2_up[i].astype(BF16),
            "w2d": w_ffn2_down[i].astype(BF16),
            "g_ple": g_ple[i], "w_ple_gate": w_ple_gate[i].astype(BF16),
            "w_ple_proj": w_ple_proj[i].astype(BF16),
        }
        xs = [_layer(x, p[i], w) for x, p in zip(xs, ps)]
    return (xs[0], xs[1])
```

```python
import functools

import numpy as np
import jax
import jax.numpy as jnp
from jax import lax
from jax.experimental import pallas as pl
from jax.experimental.pallas import tpu as pltpu

F32 = jnp.float32
BF16 = jnp.bfloat16

GRID_W = 64
WIN_H = 8
WIN_W = 16
H_A = 8
DH_A = 128
H_M = 4
DH_M = 256
W_A = H_A * DH_A
W_M = H_M * DH_M
EPS = 1e-6
N_GATES = 4 * H_M
U_MAIN = 3 * W_A + 4 * W_M

LANES = 128
V7X_VMEM_BYTES = 64 * 1024 * 1024

TM = 512
TM_FFN = 1024
TM_IN = 1024
TF = 512
TN_IN = 1024
ATT_RB = 4
ATT_WIN = 12
ATT_HEADS = 4
ML_CHUNK = 256
NEG_BIG = -1e30


def _vmem_limit(block_bytes, scratch_bytes):
    est = 2 * block_bytes + scratch_bytes + 12 * 1024 * 1024
    return int(min(est, V7X_VMEM_BYTES - 6 * 1024 * 1024))


def _nbytes(shape, dtype):
    return int(np.prod(shape)) * jnp.dtype(dtype).itemsize


def _rms_rows(x, g):
    ms = jnp.mean(x * x, axis=-1, keepdims=True)
    return x * lax.rsqrt(ms + EPS) * g


def _ffn_kernel(x_ref, g_ref, wg_ref, wu_ref, wd_ref, o_ref, hn_ref):
    j = pl.program_id(1)

    @pl.when(j == 0)
    def _():
        x = x_ref[...]
        hn_ref[...] = _rms_rows(x, g_ref[...]).astype(BF16)
        o_ref[...] = x

    hn = hn_ref[...]
    a = jnp.dot(hn, wg_ref[...], preferred_element_type=F32)
    b = jnp.dot(hn, wu_ref[...], preferred_element_type=F32)
    h = (a * jax.nn.sigmoid(a) * (0.5 * b)).astype(BF16)
    o_ref[...] += jnp.dot(h, wd_ref[...], preferred_element_type=F32)


def _ffn(x, g, wg, wu, wd):
    T, D = x.shape
    F = wg.shape[1]
    tm = TM_FFN
    blocks = (_nbytes((tm, D), F32) * 2 + _nbytes((D, TF), BF16) * 3)
    scratch = _nbytes((tm, D), BF16)
    return pl.pallas_call(
        _ffn_kernel,
        grid=(T // tm, F // TF),
        in_specs=[
            pl.BlockSpec((tm, D), lambda i, j: (i, 0)),
            pl.BlockSpec((1, D), lambda i, j: (0, 0)),
            pl.BlockSpec((D, TF), lambda i, j: (0, j)),
            pl.BlockSpec((D, TF), lambda i, j: (0, j)),
            pl.BlockSpec((TF, D), lambda i, j: (j, 0)),
        ],
        out_specs=pl.BlockSpec((tm, D), lambda i, j: (i, 0)),
        out_shape=jax.ShapeDtypeStruct((T, D), F32),
        scratch_shapes=[pltpu.VMEM((tm, D), BF16)],
        compiler_params=pltpu.CompilerParams(
            dimension_semantics=("arbitrary", "arbitrary"),
            vmem_limit_bytes=_vmem_limit(blocks, scratch)),
        name="ffn",
    )(x, g.reshape(1, D), wg, wu, wd)


def _inproj_kernel(x_ref, g_ref, w_ref, wgate_ref, gq_ref, gk_ref, u_ref, gates_ref, hn_ref):
    j = pl.program_id(1)

    @pl.when(j == 0)
    def _():
        hn = _rms_rows(x_ref[...], g_ref[...]).astype(BF16)
        hn_ref[...] = hn
        gates_ref[...] = jnp.dot(hn, wgate_ref[...], preferred_element_type=F32)

    acc = jnp.dot(hn_ref[...], w_ref[...], preferred_element_type=F32)
    u_ref[...] = acc.astype(BF16)
    n_qk_tiles = 2 * W_A // TN_IN

    @pl.when(j < n_qk_tiles)
    def _():
        is_q = j < W_A // TN_IN
        gain = jnp.where(is_q, gq_ref[...] * (DH_A ** -0.5), gk_ref[...])
        for hh in range(TN_IN // DH_A):
            sl = slice(hh * DH_A, (hh + 1) * DH_A)
            u_ref[:, sl] = _rms_rows(acc[:, sl], gain).astype(BF16)


def _inproj(x, g, w_main, w_gate, g_qn, g_kn):
    T, D = x.shape
    tm = TM_IN
    nj = U_MAIN // TN_IN
    blocks = (_nbytes((tm, D), F32) + _nbytes((D, TN_IN), BF16) + _nbytes((D, LANES), BF16)
              + _nbytes((tm, TN_IN), BF16) + _nbytes((tm, LANES), F32))
    scratch = _nbytes((tm, D), BF16)
    return pl.pallas_call(
        _inproj_kernel,
        grid=(T // tm, nj),
        in_specs=[
            pl.BlockSpec((tm, D), lambda i, j: (i, 0)),
            pl.BlockSpec((1, D), lambda i, j: (0, 0)),
            pl.BlockSpec((D, TN_IN), lambda i, j: (0, j)),
            pl.BlockSpec((D, LANES), lambda i, j: (0, 0)),
            pl.BlockSpec((1, DH_A), lambda i, j: (0, 0)),
            pl.BlockSpec((1, DH_A), lambda i, j: (0, 0)),
        ],
        out_specs=[
            pl.BlockSpec((tm, TN_IN), lambda i, j: (i, j)),
            pl.BlockSpec((tm, LANES), lambda i, j: (i, 0)),
        ],
        out_shape=[
            jax.ShapeDtypeStruct((T, U_MAIN), BF16),
            jax.ShapeDtypeStruct((T, LANES), F32),
        ],
        scratch_shapes=[pltpu.VMEM((tm, D), BF16)],
        compiler_params=pltpu.CompilerParams(
            dimension_semantics=("arbitrary", "arbitrary"),
            vmem_limit_bytes=_vmem_limit(blocks, scratch)),
        name="inproj",
    )(x, g.reshape(1, D), w_main, w_gate, g_qn.reshape(1, DH_A), g_kn.reshape(1, DH_A))


def _attn_bias_table(rpb):
    a = np.arange(ATT_RB)[:, None]
    ik = np.arange(ATT_WIN)[None, :]
    kh = WIN_H
    valid0 = ik < kh
    dr0 = ik - a + (WIN_H - 1)
    valid1 = (ik >= a) & (ik < a + kh)
    dr1 = ik - a + (WIN_H - 1) - kh // 2
    valid2 = (ik >= ATT_WIN - kh) & (ik < ATT_WIN)
    dr2 = (ik - ATT_WIN) - (a - ATT_RB) + (WIN_H - 1)
    valid = np.stack([np.broadcast_to(valid0, dr0.shape), valid1,
                      np.broadcast_to(valid2, dr0.shape)])
    dr = np.stack([dr0, dr1, dr2])

    c = np.arange(GRID_W)
    cs = np.clip(c - WIN_W // 2, 0, GRID_W - WIN_W)
    colmask = (c[None, :] >= cs[:, None]) & (c[None, :] < cs[:, None] + WIN_W)
    dc = np.clip(c[None, :] - c[:, None], -(WIN_W - 1), WIN_W - 1) + (WIN_W - 1)

    onehot = (dc[None] == np.arange(2 * WIN_W - 1)[:, None, None]).astype(np.float32)
    tiles = jnp.einsum("hrc,cqk->hrqk", rpb.astype(F32), jnp.asarray(onehot),
                       precision=lax.Precision.HIGHEST)
    tiles = jnp.where(jnp.asarray(colmask), tiles, NEG_BIG)
    masked = jnp.full((H_A, GRID_W, GRID_W), NEG_BIG, F32)
    classes = []
    for cls in range(3):
        rows = []
        for qa in range(ATT_RB):
            rows.append(jnp.concatenate(
                [tiles[:, int(dr[cls, qa, kk])] if valid[cls, qa, kk] else masked
                 for kk in range(ATT_WIN)], axis=-1))
        classes.append(jnp.concatenate(rows, axis=1))
    return jnp.stack(classes)


def _attn_kernel(q_ref, k_ref, v_ref, bias_ref, o_ref, *, n_rows):
    i = pl.program_id(2)
    nb = n_rows // ATT_RB
    cls = jnp.where(i == 0, 0, jnp.where(i == nb - 1, 2, 1))

    for hh in range(ATT_HEADS):
        sl = slice(hh * DH_A, (hh + 1) * DH_A)
        q = q_ref[0, :, sl]
        kw = k_ref[0, :, sl]
        vw = v_ref[0, :, sl]
        s = lax.dot_general(q, kw, (((1,), (1,)), ((), ())), preferred_element_type=F32)
        s = s + bias_ref[cls, hh]
        m = jnp.max(s, axis=-1, keepdims=True)
        p = jnp.exp(s - m)
        l = jnp.sum(p, axis=-1, keepdims=True)
        o = jnp.dot(p.astype(BF16), vw, preferred_element_type=F32)
        o_ref[0, :, sl] = (o / l).astype(BF16)


def _attention(u, bias, B, S):
    R = S // GRID_W
    assert R % ATT_RB == 0 and R >= ATT_WIN and R // ATT_RB >= 3
    tq = ATT_RB * GRID_W
    tk = ATT_WIN * GRID_W
    hw = ATT_HEADS * DH_A
    ng = H_A // ATT_HEADS
    blocks = (_nbytes((tq, hw), BF16) * 2 + _nbytes((tk, hw), BF16) * 2
              + _nbytes((3, ATT_HEADS, tq, tk), F32))

    def win_start(i):
        return jnp.clip(i * ATT_RB - WIN_H // 2, 0, R - ATT_WIN) * GRID_W

    return pl.pallas_call(
        functools.partial(_attn_kernel, n_rows=R),
        grid=(ng, B, R // ATT_RB),
        in_specs=[
            pl.BlockSpec((1, tq, hw), lambda h, b, i: (b, i, h)),
            pl.BlockSpec((pl.Element(1), pl.Element(tk), pl.Element(hw)),
                         lambda h, b, i: (b, win_start(i), (ng + h) * hw)),
            pl.BlockSpec((pl.Element(1), pl.Element(tk), pl.Element(hw)),
                         lambda h, b, i: (b, win_start(i), (2 * ng + h) * hw)),
            pl.BlockSpec((3, ATT_HEADS, tq, tk), lambda h, b, i: (0, h, 0, 0)),
        ],
        out_specs=pl.BlockSpec((1, tq, hw), lambda h, b, i: (b, i, h)),
        out_shape=jax.ShapeDtypeStruct((B, S, W_A), BF16),
        compiler_params=pltpu.CompilerParams(
            dimension_semantics=("arbitrary", "arbitrary", "arbitrary"),
            vmem_limit_bytes=_vmem_limit(blocks, 0)),
        name="attention",
    )(u, u, u, bias)


def _mlstm_kernel(*refs, reverse, n_chunks):
    if reverse:
        (bi_ref, bf_ref, q_ref, k_ref, v_ref, gi_ref, gf_ref, hf_ref, og_ref, gmh_ref,
         o_ref, ct_ref, n_ref, m_ref, b_ref, e_ref) = refs
    else:
        (bi_ref, bf_ref, q_ref, k_ref, v_ref, gi_ref, gf_ref,
         o_ref, ct_ref, n_ref, m_ref, b_ref, e_ref) = refs
    L = ML_CHUNK
    c = pl.program_id(1)
    d = 1 if reverse else 0

    row = lax.broadcasted_iota(jnp.int32, (L, L), 0)
    col = lax.broadcasted_iota(jnp.int32, (L, L), 1)
    tri = (col >= row) if reverse else (col <= row)
    eye = row == col

    @pl.when(c == 0)
    def _():
        ct_ref[...] = jnp.zeros_like(ct_ref)
        n_ref[...] = jnp.zeros_like(n_ref)
        m_ref[...] = jnp.zeros_like(m_ref)
        cum = jnp.where((row >= col) if reverse else (row <= col), 1.0, 0.0).astype(F32)
        for hh in range(H_M):
            ig = gi_ref[0, hh] + bi_ref[d, hh]
            lf = jax.nn.log_sigmoid(gf_ref[0, hh] + bf_ref[d, hh])
            b = jnp.dot(lf, cum, preferred_element_type=F32, precision=lax.Precision.HIGHEST)
            b_ref[hh] = b
            e_ref[hh] = ig - b

    cc = (n_chunks - 1 - c) if reverse else c

    for hh in range(H_M):
        sl = slice(hh * DH_M, (hh + 1) * DH_M)
        b_row = b_ref[hh, pl.ds(cc, 1), :]
        e_row = e_ref[hh, pl.ds(cc, 1), :]
        b_col = jnp.sum(jnp.where(eye, b_row, 0.0), axis=1, keepdims=True)
        e_col = jnp.sum(jnp.where(eye, e_row, 0.0), axis=1, keepdims=True)
        a_tot = b_row[:, 0:1] if reverse else b_row[:, L - 1:L]
        m_prev = m_ref[hh]

        q = q_ref[0, :, sl]
        k = k_ref[0, :, sl] * jnp.asarray(DH_M ** -0.5, BF16)
        v = v_ref[0, :, sl]

        dmat = jnp.where(tri, b_col + e_row, -jnp.inf)
        inter = b_col + m_prev
        m_t = jnp.maximum(inter, jnp.max(dmat, axis=1, keepdims=True))
        dexp = jnp.exp(dmat - m_t)
        s = lax.dot_general(q, k, (((1,), (1,)), ((), ())), preferred_element_type=F32)
        sqk = s * dexp
        sc = jnp.exp(inter - m_t)
        ct = ct_ref[hh]
        n_prev = n_ref[hh]
        num = sc * jnp.dot(q, ct.astype(BF16), preferred_element_type=F32) + jnp.dot(
            sqk.astype(BF16), v, preferred_element_type=F32)
        den = (sc * jnp.sum(q.astype(F32) * n_prev, axis=1, keepdims=True)
               + jnp.sum(sqk, axis=1, keepdims=True))
        hdir = num / jnp.maximum(jnp.abs(den), jnp.exp(-m_t))

        g_col = a_tot + e_col
        m_loc = jnp.max(g_col, axis=0, keepdims=True)
        m_new = jnp.maximum(a_tot + m_prev, m_loc)
        s_prev = jnp.exp(a_tot + m_prev - m_new)
        w_col = jnp.exp(g_col - m_new)
        vw = (v.astype(F32) * w_col).astype(BF16)
        ct_ref[hh] = s_prev * ct + lax.dot_general(
            k, vw, (((0,), (0,)), ((), ())), preferred_element_type=F32)
        n_ref[hh] = s_prev * n_prev + jnp.sum(k.astype(F32) * w_col, axis=0, keepdims=True)
        m_ref[hh] = m_new

        if reverse:
            hm = _rms_rows(hf_ref[0, :, sl] + hdir, gmh_ref[:, sl])
            o_ref[0, :, sl] = (jax.nn.sigmoid(og_ref[0, :, sl].astype(F32)) * hm).astype(BF16)
        else:
            o_ref[0, :, sl] = hdir


def _mlstm_sweep(u, gates_t, b_igate, b_fgate, B, S, *, reverse, hf=None, g_mh=None):
    L = ML_CHUNK
    N = S // L
    d = 1 if reverse else 0
    qoff = 3 * W_A // W_M
    cidx = (lambda c: N - 1 - c) if reverse else (lambda c: c)
    smem = pl.BlockSpec(memory_space=pltpu.SMEM)
    in_specs = [
        smem, smem,
        pl.BlockSpec((1, L, W_M), lambda b, c: (b, cidx(c), qoff)),
        pl.BlockSpec((1, L, W_M), lambda b, c: (b, cidx(c), qoff + 1)),
        pl.BlockSpec((1, L, W_M), lambda b, c: (b, cidx(c), qoff + 2)),
        pl.BlockSpec((1, H_M, N, L), lambda b, c: (b, d, 0, 0)),
        pl.BlockSpec((1, H_M, N, L), lambda b, c: (b, 2 + d, 0, 0)),
    ]
    args = [b_igate, b_fgate, u, u, u, gates_t, gates_t]
    if reverse:
        in_specs += [
            pl.BlockSpec((1, L, W_M), lambda b, c: (b, cidx(c), 0)),
            pl.BlockSpec((1, L, W_M), lambda b, c: (b, cidx(c), qoff + 3)),
            pl.BlockSpec((1, W_M), lambda b, c: (0, 0)),
        ]
        args += [hf, u, g_mh.reshape(1, W_M)]
        out_dtype = BF16
    else:
        out_dtype = F32
    blocks = (_nbytes((L, W_M), BF16) * 4 + _nbytes((H_M, N, L), F32) * 2 + _nbytes((L, W_M), F32) * 2)
    scratch = (H_M * _nbytes((DH_M, DH_M), F32) + 2 * _nbytes((H_M, N, L), F32)
               + 2 * H_M * 8 * LANES * 4)
    return pl.pallas_call(
        functools.partial(_mlstm_kernel, reverse=reverse, n_chunks=N),
        grid=(B, N),
        in_specs=in_specs,
        out_specs=pl.BlockSpec((1, L, W_M), lambda b, c: (b, cidx(c), 0)),
        out_shape=jax.ShapeDtypeStruct((B, S, W_M), out_dtype),
        scratch_shapes=[
            pltpu.VMEM((H_M, DH_M, DH_M), F32),
            pltpu.VMEM((H_M, 1, DH_M), F32),
            pltpu.VMEM((H_M, 1, 1), F32),
            pltpu.VMEM((H_M, N, L), F32),
            pltpu.VMEM((H_M, N, L), F32),
        ],
        compiler_params=pltpu.CompilerParams(
            dimension_semantics=("arbitrary", "arbitrary"),
            vmem_limit_bytes=_vmem_limit(blocks, scratch)),
        name="mlstm_bwd" if reverse else "mlstm_fwd",
    )(*args)


def _outproj_kernel(x_ref, ya_ref, ym_ref, wa_ref, wm_ref, o_ref):
    o_ref[...] = (x_ref[...]
                  + jnp.dot(ya_ref[...], wa_ref[...], preferred_element_type=F32)
                  + jnp.dot(ym_ref[...], wm_ref[...], preferred_element_type=F32))


def _outproj(x, ya, ym, w_a, w_m):
    T, D = x.shape
    blocks = (_nbytes((TM, D), F32) * 2 + _nbytes((TM, W_A), BF16) * 2 + _nbytes((W_A, D), BF16) * 2)
    return pl.pallas_call(
        _outproj_kernel,
        grid=(T // TM,),
        in_specs=[
            pl.BlockSpec((TM, D), lambda i: (i, 0)),
            pl.BlockSpec((TM, W_A), lambda i: (i, 0)),
            pl.BlockSpec((TM, W_M), lambda i: (i, 0)),
            pl.BlockSpec((W_A, D), lambda i: (0, 0)),
            pl.BlockSpec((W_M, D), lambda i: (0, 0)),
        ],
        out_specs=pl.BlockSpec((TM, D), lambda i: (i, 0)),
        out_shape=jax.ShapeDtypeStruct((T, D), F32),
        compiler_params=pltpu.CompilerParams(
            dimension_semantics=("arbitrary",),
            vmem_limit_bytes=_vmem_limit(blocks, 0)),
        name="outproj",
    )(x, ya, ym, w_a, w_m)


def _ple_kernel(x_ref, g_ref, pe_ref, wg_ref, wp_ref, o_ref):
    x = x_ref[...]
    hn = _rms_rows(x, g_ref[...]).astype(BF16)
    gate = jax.nn.sigmoid(jnp.dot(hn, wg_ref[...], preferred_element_type=F32))
    proj = jnp.dot(pe_ref[...].astype(BF16), wp_ref[...], preferred_element_type=F32)
    o_ref[...] = x + gate * proj


def _ple(x, g, pe, w_gate, w_proj):
    T, D = x.shape
    P = pe.shape[1]
    blocks = (_nbytes((TM, D), F32) * 2 + _nbytes((TM, P), F32) + _nbytes((D, D), BF16)
              + _nbytes((P, D), BF16))
    return pl.pallas_call(
        _ple_kernel,
        grid=(T // TM,),
        in_specs=[
            pl.BlockSpec((TM, D), lambda i: (i, 0)),
            pl.BlockSpec((1, D), lambda i: (0, 0)),
            pl.BlockSpec((TM, P), lambda i: (i, 0)),
            pl.BlockSpec((D, D), lambda i: (0, 0)),
            pl.BlockSpec((P, D), lambda i: (0, 0)),
        ],
        out_specs=pl.BlockSpec((TM, D), lambda i: (i, 0)),
        out_shape=jax.ShapeDtypeStruct((T, D), F32),
        compiler_params=pltpu.CompilerParams(
            dimension_semantics=("arbitrary",),
            vmem_limit_bytes=_vmem_limit(blocks, 0)),
        name="ple",
    )(x, g.reshape(1, D), pe, w_gate, w_proj)


def _layer(x, pe, w):
    B, S, D = x.shape
    T = B * S
    x2 = x.reshape(T, D)
    x2 = _ffn(x2, w["g_ffn1"], w["w1g"], w["w1u"], w["w1d"])

    u, gates = _inproj(x2, w["g_mix"], w["w_in_main"], w["w_in_gate"], w["g_qn"], w["g_kn"])
    u = u.reshape(B, S, U_MAIN)
    ya = _attention(u, w["attn_bias"], B, S)

    n_chunks = S // ML_CHUNK
    gates_t = jnp.transpose(gates[:, :N_GATES].reshape(B, S, N_GATES), (0, 2, 1))
    gates_t = gates_t.reshape(B, N_GATES, n_chunks, ML_CHUNK)
    hf = _mlstm_sweep(u, gates_t, w["b_igate"], w["b_fgate"], B, S, reverse=False)
    ym = _mlstm_sweep(u, gates_t, w["b_igate"], w["b_fgate"], B, S, reverse=True,
                      hf=hf, g_mh=w["g_mh"])

    x2 = _outproj(x2, ya.reshape(T, W_A), ym.reshape(T, W_M), w["w_out_a"], w["w_out_m"])
    x2 = _ffn(x2, w["g_ffn2"], w["w2g"], w["w2u"], w["w2d"])
    x2 = _ple(x2, w["g_ple"], pe.reshape(T, -1), w["w_ple_gate"], w["w_ple_proj"])
    return x2.reshape(B, S, D)


def kernel(x_prompt, x_sample, p_prompt, p_sample, g_ffn1, w_ffn1_gate, w_ffn1_up, w_ffn1_down, g_mix, w_in, b_igate, b_fgate, g_qn, g_kn, rpb, g_mh, w_out, g_ffn2, w_ffn2_gate, w_ffn2_up, w_ffn2_down, g_ple, w_ple_gate, w_ple_proj):
    depth = g_ffn1.shape[0]
    xs = [x_prompt, x_sample]
    ps = [p_prompt, p_sample]
    for i in range(depth):
        w_in_i = w_in[i]
        gate_cols = jnp.pad(w_in_i[:, U_MAIN:], ((0, 0), (0, LANES - N_GATES)))
        w = {
            "g_ffn1": g_ffn1[i], "w1g": w_ffn1_gate[i].astype(BF16), "w1u": w_ffn1_up[i].astype(BF16),
            "w1d": w_ffn1_down[i].astype(BF16),
            "g_mix": g_mix[i], "w_in_main": w_in_i[:, :U_MAIN].astype(BF16),
            "w_in_gate": gate_cols.astype(BF16),
            "b_igate": b_igate[i], "b_fgate": b_fgate[i], "g_qn": g_qn[i], "g_kn": g_kn[i],
            "attn_bias": _attn_bias_table(rpb[i]), "g_mh": g_mh[i],
            "w_out_a": w_out[i, :W_A].astype(BF16), "w_out_m": w_out[i, W_A:].astype(BF16),
            "g_ffn2": g_ffn2[i], "w2g": w_ffn2_gate[i].astype(BF16), "w2u": w_ffn2_up[i].astype(BF16),
            "w2d": w_ffn2_down[i].astype(BF16),
            "g_ple": g_ple[i], "w_ple_gate": w_ple_gate[i].astype(BF16),
            "w_ple_proj": w_ple_proj[i].astype(BF16),
        }
        xs = [_layer(x, p[i], w) for x, p in zip(xs, ps)]
    return (xs[0], xs[1])
```

```python
import functools

import numpy as np
import jax
import jax.numpy as jnp
from jax import lax
from jax.experimental import pallas as pl
from jax.experimental.pallas import tpu as pltpu

F32 = jnp.float32
BF16 = jnp.bfloat16

GRID_W = 64
WIN_H = 8
WIN_W = 16
H_A = 8
DH_A = 128
H_M = 4
DH_M = 256
W_A = H_A * DH_A
W_M = H_M * DH_M
EPS = 1e-6
N_GATES = 4 * H_M
U_MAIN = 3 * W_A + 3 * W_M

LANES = 128
V7X_VMEM_BYTES = 64 * 1024 * 1024

TM = 512
TM_FFN = 1024
TM_IN = 1024
TF = 512
TN_IN = 1024
ATT_RB = 4
ATT_WIN = 12
ATT_HEADS = 4
ML_CHUNK = 256
NEG_BIG = -1e30


def _vmem_limit(block_bytes, scratch_bytes):
    est = 2 * block_bytes + scratch_bytes + 12 * 1024 * 1024
    return int(min(est, V7X_VMEM_BYTES - 6 * 1024 * 1024))


def _nbytes(shape, dtype):
    return int(np.prod(shape)) * jnp.dtype(dtype).itemsize


def _rms_rows(x, g):
    ms = jnp.mean(x * x, axis=-1, keepdims=True)
    return x * lax.rsqrt(ms + EPS) * g


def _ffn_kernel(x_ref, g_ref, wg_ref, wu_ref, wd_ref, o_ref, hn_ref):
    j = pl.program_id(1)

    def half_step(hn):
        a = jnp.dot(hn, wg_ref[...], preferred_element_type=F32)
        b = jnp.dot(hn, wu_ref[...], preferred_element_type=F32)
        h = (a * jax.nn.sigmoid(a) * (0.5 * b)).astype(BF16)
        return jnp.dot(h, wd_ref[...], preferred_element_type=F32)

    @pl.when(j == 0)
    def _():
        x = x_ref[...]
        hn = _rms_rows(x, g_ref[...]).astype(BF16)
        hn_ref[...] = hn
        o_ref[...] = x + half_step(hn)

    @pl.when(j > 0)
    def _():
        o_ref[...] += half_step(hn_ref[...])


def _ffn(x, g, wg, wu, wd):
    T, D = x.shape
    F = wg.shape[1]
    tm = TM_FFN
    blocks = (_nbytes((tm, D), F32) * 2 + _nbytes((D, TF), BF16) * 3)
    scratch = _nbytes((tm, D), BF16)
    return pl.pallas_call(
        _ffn_kernel,
        grid=(T // tm, F // TF),
        in_specs=[
            pl.BlockSpec((tm, D), lambda i, j: (i, 0)),
            pl.BlockSpec((1, D), lambda i, j: (0, 0)),
            pl.BlockSpec((D, TF), lambda i, j: (0, j)),
            pl.BlockSpec((D, TF), lambda i, j: (0, j)),
            pl.BlockSpec((TF, D), lambda i, j: (j, 0)),
        ],
        out_specs=pl.BlockSpec((tm, D), lambda i, j: (i, 0)),
        out_shape=jax.ShapeDtypeStruct((T, D), F32),
        scratch_shapes=[pltpu.VMEM((tm, D), BF16)],
        compiler_params=pltpu.CompilerParams(
            dimension_semantics=("arbitrary", "arbitrary"),
            vmem_limit_bytes=_vmem_limit(blocks, scratch)),
        name="ffn",
    )(x, g.reshape(1, D), wg, wu, wd)


def _inproj_kernel(x_ref, g_ref, w_ref, wgate_ref, wkt_ref, gq_ref, gk_ref,
                   u_ref, gates_ref, kt_ref, hn_ref, *, n_u_tiles):
    j = pl.program_id(1)

    n_q_tiles = W_A // TN_IN
    n_qk_tiles = 2 * n_q_tiles

    def project(hn):
        return jnp.dot(hn, w_ref[...], preferred_element_type=F32)

    def store_qk_normed(acc, gain):
        for hh in range(TN_IN // DH_A):
            sl = slice(hh * DH_A, (hh + 1) * DH_A)
            u_ref[:, sl] = _rms_rows(acc[:, sl], gain).astype(BF16)

    @pl.when(j == 0)
    def _():
        hn = _rms_rows(x_ref[...], g_ref[...]).astype(BF16)
        hn_ref[...] = hn
        gates_ref[...] = jnp.dot(hn, wgate_ref[...], preferred_element_type=F32)
        store_qk_normed(project(hn), gq_ref[...] * (DH_A ** -0.5))

    @pl.when((j > 0) & (j < n_qk_tiles))
    def _():
        gain = jnp.where(j < n_q_tiles, gq_ref[...] * (DH_A ** -0.5), gk_ref[...])
        store_qk_normed(project(hn_ref[...]), gain)

    @pl.when((j >= n_qk_tiles) & (j < n_u_tiles))
    def _():
        u_ref[...] = project(hn_ref[...]).astype(BF16)

    @pl.when(j == n_u_tiles)
    def _():
        kt_ref[...] = lax.dot_general(wkt_ref[...], hn_ref[...], (((1,), (1,)), ((), ())),
                                      preferred_element_type=F32).astype(BF16)


def _inproj(x, g, w_main, w_gate, w_kt, g_qn, g_kn):
    T, D = x.shape
    tm = TM_IN
    nu = U_MAIN // TN_IN
    blocks = (_nbytes((tm, D), F32) + _nbytes((D, TN_IN), BF16) + _nbytes((D, LANES), BF16)
              + _nbytes((W_M, D), BF16) + _nbytes((tm, TN_IN), BF16) + _nbytes((tm, LANES), F32)
              + _nbytes((W_M, tm), BF16))
    scratch = _nbytes((tm, D), BF16)
    last = nu - 1
    return pl.pallas_call(
        functools.partial(_inproj_kernel, n_u_tiles=nu),
        grid=(T // tm, nu + 1),
        in_specs=[
            pl.BlockSpec((tm, D), lambda i, j: (i, 0)),
            pl.BlockSpec((1, D), lambda i, j: (0, 0)),
            pl.BlockSpec((D, TN_IN), lambda i, j: (0, jnp.minimum(j, last))),
            pl.BlockSpec((D, LANES), lambda i, j: (0, 0)),
            pl.BlockSpec((W_M, D), lambda i, j: (0, 0)),
            pl.BlockSpec((1, DH_A), lambda i, j: (0, 0)),
            pl.BlockSpec((1, DH_A), lambda i, j: (0, 0)),
        ],
        out_specs=[
            pl.BlockSpec((tm, TN_IN), lambda i, j: (i, jnp.minimum(j, last))),
            pl.BlockSpec((tm, LANES), lambda i, j: (i, 0)),
            pl.BlockSpec((W_M, tm), lambda i, j: (0, i)),
        ],
        out_shape=[
            jax.ShapeDtypeStruct((T, U_MAIN), BF16),
            jax.ShapeDtypeStruct((T, LANES), F32),
            jax.ShapeDtypeStruct((W_M, T), BF16),
        ],
        scratch_shapes=[pltpu.VMEM((tm, D), BF16)],
        compiler_params=pltpu.CompilerParams(
            dimension_semantics=("arbitrary", "arbitrary"),
            vmem_limit_bytes=_vmem_limit(blocks, scratch)),
        name="inproj",
    )(x, g.reshape(1, D), w_main, w_gate, w_kt, g_qn.reshape(1, DH_A), g_kn.reshape(1, DH_A))


def _attn_bias_table(rpb):
    a = np.arange(ATT_RB)[:, None]
    ik = np.arange(ATT_WIN)[None, :]
    kh = WIN_H
    valid0 = ik < kh
    dr0 = ik - a + (WIN_H - 1)
    valid1 = (ik >= a) & (ik < a + kh)
    dr1 = ik - a + (WIN_H - 1) - kh // 2
    valid2 = (ik >= ATT_WIN - kh) & (ik < ATT_WIN)
    dr2 = (ik - ATT_WIN) - (a - ATT_RB) + (WIN_H - 1)
    valid = np.stack([np.broadcast_to(valid0, dr0.shape), valid1,
                      np.broadcast_to(valid2, dr0.shape)])
    dr = np.stack([dr0, dr1, dr2])

    c = np.arange(GRID_W)
    cs = np.clip(c - WIN_W // 2, 0, GRID_W - WIN_W)
    colmask = (c[None, :] >= cs[:, None]) & (c[None, :] < cs[:, None] + WIN_W)
    dc = np.clip(c[None, :] - c[:, None], -(WIN_W - 1), WIN_W - 1) + (WIN_W - 1)

    onehot = (dc[None] == np.arange(2 * WIN_W - 1)[:, None, None]).astype(np.float32)
    tiles = jnp.einsum("hrc,cqk->hrqk", rpb.astype(F32), jnp.asarray(onehot),
                       precision=lax.Precision.HIGHEST)
    tiles = jnp.where(jnp.asarray(colmask), tiles, NEG_BIG)
    masked = jnp.full((H_A, GRID_W, GRID_W), NEG_BIG, F32)
    classes = []
    for cls in range(3):
        rows = []
        for qa in range(ATT_RB):
            rows.append(jnp.concatenate(
                [tiles[:, int(dr[cls, qa, kk])] if valid[cls, qa, kk] else masked
                 for kk in range(ATT_WIN)], axis=-1))
        classes.append(jnp.concatenate(rows, axis=1))
    return jnp.stack(classes)


def _attn_kernel(q_ref, k_ref, v_ref, bias_ref, o_ref, *, n_rows):
    i = pl.program_id(2)
    nb = n_rows // ATT_RB
    cls = jnp.where(i == 0, 0, jnp.where(i == nb - 1, 2, 1))

    for hh in range(ATT_HEADS):
        sl = slice(hh * DH_A, (hh + 1) * DH_A)
        q = q_ref[0, :, sl]
        kw = k_ref[0, :, sl]
        vw = v_ref[0, :, sl]
        s = lax.dot_general(q, kw, (((1,), (1,)), ((), ())), preferred_element_type=F32)
        s = s + bias_ref[cls, hh]
        m = jnp.max(s, axis=-1, keepdims=True)
        p = jnp.exp(s - m)
        l = jnp.sum(p, axis=-1, keepdims=True)
        o = jnp.dot(p.astype(BF16), vw, preferred_element_type=F32)
        o_ref[0, :, sl] = (o / l).astype(BF16)


def _attention(u, bias, B, S):
    R = S // GRID_W
    assert R % ATT_RB == 0 and R >= ATT_WIN and R // ATT_RB >= 3
    tq = ATT_RB * GRID_W
    tk = ATT_WIN * GRID_W
    hw = ATT_HEADS * DH_A
    ng = H_A // ATT_HEADS
    blocks = (_nbytes((tq, hw), BF16) * 2 + _nbytes((tk, hw), BF16) * 2
              + _nbytes((3, ATT_HEADS, tq, tk), F32))

    def win_start(i):
        return jnp.clip(i * ATT_RB - WIN_H // 2, 0, R - ATT_WIN) * GRID_W

    return pl.pallas_call(
        functools.partial(_attn_kernel, n_rows=R),
        grid=(ng, B, R // ATT_RB),
        in_specs=[
            pl.BlockSpec((1, tq, hw), lambda h, b, i: (b, i, h)),
            pl.BlockSpec((pl.Element(1), pl.Element(tk), pl.Element(hw)),
                         lambda h, b, i: (b, win_start(i), (ng + h) * hw)),
            pl.BlockSpec((pl.Element(1), pl.Element(tk), pl.Element(hw)),
                         lambda h, b, i: (b, win_start(i), (2 * ng + h) * hw)),
            pl.BlockSpec((3, ATT_HEADS, tq, tk), lambda h, b, i: (0, h, 0, 0)),
        ],
        out_specs=pl.BlockSpec((1, tq, hw), lambda h, b, i: (b, i, h)),
        out_shape=jax.ShapeDtypeStruct((B, S, W_A), BF16),
        compiler_params=pltpu.CompilerParams(
            dimension_semantics=("arbitrary", "arbitrary", "arbitrary"),
            vmem_limit_bytes=_vmem_limit(blocks, 0)),
        name="attention",
    )(u, u, u, bias)


def _mlstm_kernel(*refs, reverse, n_chunks):
    if reverse:
        (bi_ref, bf_ref, q_ref, kt_ref, v_ref, gi_ref, gf_ref, hf_ref, og_ref, gmh_ref,
         o_ref, ct_ref, m_ref, b_ref, e_ref) = refs
    else:
        (bi_ref, bf_ref, q_ref, kt_ref, v_ref, gi_ref, gf_ref,
         o_ref, ct_ref, m_ref, b_ref, e_ref) = refs
    L = ML_CHUNK
    c = pl.program_id(1)
    d = 1 if reverse else 0

    row = lax.broadcasted_iota(jnp.int32, (L, L), 0)
    col = lax.broadcasted_iota(jnp.int32, (L, L), 1)
    tri = (col >= row) if reverse else (col <= row)
    eye = row == col

    @pl.when(c == 0)
    def _():
        ct_ref[...] = jnp.zeros_like(ct_ref)
        m_ref[...] = jnp.zeros_like(m_ref)
        cum = jnp.where((row >= col) if reverse else (row <= col), 1.0, 0.0).astype(F32)
        for hh in range(H_M):
            ig = gi_ref[0, hh] + bi_ref[d, hh]
            lf = jax.nn.log_sigmoid(gf_ref[0, hh] + bf_ref[d, hh])
            b = jnp.dot(lf, cum, preferred_element_type=F32, precision=lax.Precision.HIGHEST)
            b_ref[hh] = b
            e_ref[hh] = ig - b

    cc = (n_chunks - 1 - c) if reverse else c

    for hh in range(H_M):
        sl = slice(hh * DH_M, (hh + 1) * DH_M)
        b_row = b_ref[hh, pl.ds(cc, 1), :]
        e_row = e_ref[hh, pl.ds(cc, 1), :]
        b_col = jnp.sum(jnp.where(eye, b_row, 0.0), axis=1, keepdims=True)
        e_col = jnp.sum(jnp.where(eye, e_row, 0.0), axis=1, keepdims=True)
        a_tot = b_row[:, 0:1] if reverse else b_row[:, L - 1:L]
        m_prev = m_ref[hh]

        q = q_ref[0, :, sl]
        kt = kt_ref[sl, :]
        v = v_ref[0, :, sl]

        dmat = jnp.where(tri, b_col + e_row, -jnp.inf)
        inter = b_col + m_prev
        m_t = jnp.maximum(inter, jnp.max(dmat, axis=1, keepdims=True))
        dexp = jnp.exp(dmat - m_t)
        s = jnp.dot(q, kt, preferred_element_type=F32)
        sqk = s * dexp
        sc = jnp.exp(inter - m_t)
        ct = ct_ref[hh]
        inter_aug = jnp.dot(q, ct.astype(BF16), preferred_element_type=F32)
        num = sc * inter_aug[:, :DH_M] + jnp.dot(sqk.astype(BF16), v, preferred_element_type=F32)
        den = sc * inter_aug[:, DH_M:] + jnp.sum(sqk, axis=1, keepdims=True)
        rden = 1.0 / jnp.maximum(jnp.abs(den), jnp.exp(-m_t))
        hdir = num * jnp.concatenate([rden] * (DH_M // LANES), axis=1)

        g_col = a_tot + e_col
        m_loc = jnp.max(g_col, axis=0, keepdims=True)
        m_new = jnp.maximum(a_tot + m_prev, m_loc)
        s_prev = jnp.exp(a_tot + m_prev - m_new)
        w_col = jnp.exp(g_col - m_new)
        vw_aug = jnp.concatenate(
            [(v.astype(F32) * w_col).astype(BF16),
             jnp.broadcast_to(w_col, (L, LANES)).astype(BF16)], axis=1)
        ct_ref[hh] = s_prev * ct + jnp.dot(kt, vw_aug, preferred_element_type=F32)
        m_ref[hh] = m_new

        if reverse:
            hm = _rms_rows(hf_ref[0, :, sl] + hdir, gmh_ref[:, sl])
            o_ref[0, :, sl] = (jax.nn.sigmoid(og_ref[0, :, sl].astype(F32)) * hm).astype(BF16)
        else:
            o_ref[0, :, sl] = hdir


def _mlstm_sweep(u, kt, gates_t, b_igate, b_fgate, B, S, *, reverse, hf=None, g_mh=None):
    L = ML_CHUNK
    N = S // L
    d = 1 if reverse else 0
    qoff = 3 * W_A // W_M
    cidx = (lambda c: N - 1 - c) if reverse else (lambda c: c)
    smem = pl.BlockSpec(memory_space=pltpu.SMEM)
    in_specs = [
        smem, smem,
        pl.BlockSpec((1, L, W_M), lambda b, c: (b, cidx(c), qoff)),
        pl.BlockSpec((W_M, L), lambda b, c: (0, b * N + cidx(c))),
        pl.BlockSpec((1, L, W_M), lambda b, c: (b, cidx(c), qoff + 1)),
        pl.BlockSpec((1, H_M, N, L), lambda b, c: (b, d, 0, 0)),
        pl.BlockSpec((1, H_M, N, L), lambda b, c: (b, 2 + d, 0, 0)),
    ]
    args = [b_igate, b_fgate, u, kt, u, gates_t, gates_t]
    if reverse:
        in_specs += [
            pl.BlockSpec((1, L, W_M), lambda b, c: (b, cidx(c), 0)),
            pl.BlockSpec((1, L, W_M), lambda b, c: (b, cidx(c), qoff + 2)),
            pl.BlockSpec((1, W_M), lambda b, c: (0, 0)),
        ]
        args += [hf, u, g_mh.reshape(1, W_M)]
        out_dtype = BF16
    else:
        out_dtype = F32
    blocks = (_nbytes((L, W_M), BF16) * 4 + _nbytes((H_M, N, L), F32) * 2 + _nbytes((L, W_M), F32) * 2)
    scratch = (H_M * _nbytes((DH_M, DH_M + LANES), F32) + 2 * _nbytes((H_M, N, L), F32)
               + H_M * 8 * LANES * 4)
    return pl.pallas_call(
        functools.partial(_mlstm_kernel, reverse=reverse, n_chunks=N),
        grid=(B, N),
        in_specs=in_specs,
        out_specs=pl.BlockSpec((1, L, W_M), lambda b, c: (b, cidx(c), 0)),
        out_shape=jax.ShapeDtypeStruct((B, S, W_M), out_dtype),
        scratch_shapes=[
            pltpu.VMEM((H_M, DH_M, DH_M + LANES), F32),
            pltpu.VMEM((H_M, 1, 1), F32),
            pltpu.VMEM((H_M, N, L), F32),
            pltpu.VMEM((H_M, N, L), F32),
        ],
        compiler_params=pltpu.CompilerParams(
            dimension_semantics=("arbitrary", "arbitrary"),
            vmem_limit_bytes=_vmem_limit(blocks, scratch)),
        name="mlstm_bwd" if reverse else "mlstm_fwd",
    )(*args)


def _outproj_kernel(x_ref, ya_ref, ym_ref, wa_ref, wm_ref, o_ref):
    o_ref[...] = (x_ref[...]
                  + jnp.dot(ya_ref[...], wa_ref[...], preferred_element_type=F32)
                  + jnp.dot(ym_ref[...], wm_ref[...], preferred_element_type=F32))


def _outproj(x, ya, ym, w_a, w_m):
    T, D = x.shape
    blocks = (_nbytes((TM, D), F32) * 2 + _nbytes((TM, W_A), BF16) * 2 + _nbytes((W_A, D), BF16) * 2)
    return pl.pallas_call(
        _outproj_kernel,
        grid=(T // TM,),
        in_specs=[
            pl.BlockSpec((TM, D), lambda i: (i, 0)),
            pl.BlockSpec((TM, W_A), lambda i: (i, 0)),
            pl.BlockSpec((TM, W_M), lambda i: (i, 0)),
            pl.BlockSpec((W_A, D), lambda i: (0, 0)),
            pl.BlockSpec((W_M, D), lambda i: (0, 0)),
        ],
        out_specs=pl.BlockSpec((TM, D), lambda i: (i, 0)),
        out_shape=jax.ShapeDtypeStruct((T, D), F32),
        compiler_params=pltpu.CompilerParams(
            dimension_semantics=("arbitrary",),
            vmem_limit_bytes=_vmem_limit(blocks, 0)),
        name="outproj",
    )(x, ya, ym, w_a, w_m)


def _ple_kernel(x_ref, g_ref, pe_ref, wg_ref, wp_ref, o_ref):
    x = x_ref[...]
    hn = _rms_rows(x, g_ref[...]).astype(BF16)
    gate = jax.nn.sigmoid(jnp.dot(hn, wg_ref[...], preferred_element_type=F32))
    proj = jnp.dot(pe_ref[...].astype(BF16), wp_ref[...], preferred_element_type=F32)
    o_ref[...] = x + gate * proj


def _ple(x, g, pe, w_gate, w_proj):
    T, D = x.shape
    P = pe.shape[1]
    blocks = (_nbytes((TM, D), F32) * 2 + _nbytes((TM, P), F32) + _nbytes((D, D), BF16)
              + _nbytes((P, D), BF16))
    return pl.pallas_call(
        _ple_kernel,
        grid=(T // TM,),
        in_specs=[
            pl.BlockSpec((TM, D), lambda i: (i, 0)),
            pl.BlockSpec((1, D), lambda i: (0, 0)),
            pl.BlockSpec((TM, P), lambda i: (i, 0)),
            pl.BlockSpec((D, D), lambda i: (0, 0)),
            pl.BlockSpec((P, D), lambda i: (0, 0)),
        ],
        out_specs=pl.BlockSpec((TM, D), lambda i: (i, 0)),
        out_shape=jax.ShapeDtypeStruct((T, D), F32),
        compiler_params=pltpu.CompilerParams(
            dimension_semantics=("arbitrary",),
            vmem_limit_bytes=_vmem_limit(blocks, 0)),
        name="ple",
    )(x, g.reshape(1, D), pe, w_gate, w_proj)


def _layer(x, pe, w):
    B, S, D = x.shape
    T = B * S
    x2 = x.reshape(T, D)
    x2 = _ffn(x2, w["g_ffn1"], w["w1g"], w["w1u"], w["w1d"])

    u, gates, kt = _inproj(x2, w["g_mix"], w["w_in_main"], w["w_in_gate"], w["w_in_kt"],
                           w["g_qn"], w["g_kn"])
    u = u.reshape(B, S, U_MAIN)
    ya = _attention(u, w["attn_bias"], B, S)

    n_chunks = S // ML_CHUNK
    gates_t = jnp.transpose(gates[:, :N_GATES].reshape(B, S, N_GATES), (0, 2, 1))
    gates_t = gates_t.reshape(B, N_GATES, n_chunks, ML_CHUNK)
    hf = _mlstm_sweep(u, kt, gates_t, w["b_igate"], w["b_fgate"], B, S, reverse=False)
    ym = _mlstm_sweep(u, kt, gates_t, w["b_igate"], w["b_fgate"], B, S, reverse=True,
                      hf=hf, g_mh=w["g_mh"])

    x2 = _outproj(x2, ya.reshape(T, W_A), ym.reshape(T, W_M), w["w_out_a"], w["w_out_m"])
    x2 = _ffn(x2, w["g_ffn2"], w["w2g"], w["w2u"], w["w2d"])
    x2 = _ple(x2, w["g_ple"], pe.reshape(T, -1), w["w_ple_gate"], w["w_ple_proj"])
    return x2.reshape(B, S, D)


def kernel(x_prompt, x_sample, p_prompt, p_sample, g_ffn1, w_ffn1_gate, w_ffn1_up, w_ffn1_down, g_mix, w_in, b_igate, b_fgate, g_qn, g_kn, rpb, g_mh, w_out, g_ffn2, w_ffn2_gate, w_ffn2_up, w_ffn2_down, g_ple, w_ple_gate, w_ple_proj):
    depth = g_ffn1.shape[0]
    xs = [x_prompt, x_sample]
    ps = [p_prompt, p_sample]
    for i in range(depth):
        w_in_i = w_in[i]
        k_lo = 3 * W_A + W_M
        k_hi = k_lo + W_M
        u_end = 3 * W_A + 4 * W_M
        gate_cols = jnp.pad(w_in_i[:, u_end:], ((0, 0), (0, LANES - N_GATES)))
        w = {
            "g_ffn1": g_ffn1[i], "w1g": w_ffn1_gate[i].astype(BF16), "w1u": w_ffn1_up[i].astype(BF16),
            "w1d": w_ffn1_down[i].astype(BF16),
            "g_mix": g_mix[i],
            "w_in_main": jnp.concatenate([w_in_i[:, :k_lo], w_in_i[:, k_hi:u_end]], axis=1).astype(BF16),
            "w_in_gate": gate_cols.astype(BF16),
            "w_in_kt": (w_in_i[:, k_lo:k_hi] * (DH_M ** -0.5)).T.astype(BF16),
            "b_igate": b_igate[i], "b_fgate": b_fgate[i], "g_qn": g_qn[i], "g_kn": g_kn[i],
            "attn_bias": _attn_bias_table(rpb[i]), "g_mh": g_mh[i],
            "w_out_a": w_out[i, :W_A].astype(BF16), "w_out_m": w_out[i, W_A:].astype(BF16),
            "g_ffn2": g_ffn2[i], "w2g": w_ffn2_gate[i].astype(BF16), "w2u": w_ffn2_up[i].astype(BF16),
            "w2d": w_ffn2_down[i].astype(BF16),
            "g_ple": g_ple[i], "w_ple_gate": w_ple_gate[i].astype(BF16),
            "w_ple_proj": w_ple_proj[i].astype(BF16),
        }
        xs = [_layer(x, p[i], w) for x, p in zip(xs, ps)]
    return (xs[0], xs[1])
```

```python
import functools

import numpy as np
import jax
import jax.numpy as jnp
from jax import lax
from jax.experimental import pallas as pl
from jax.experimental.pallas import tpu as pltpu

F32 = jnp.float32
BF16 = jnp.bfloat16

GRID_W = 64
WIN_H = 8
WIN_W = 16
H_A = 8
DH_A = 128
H_M = 4
DH_M = 256
W_A = H_A * DH_A
W_M = H_M * DH_M
EPS = 1e-6
N_GATES = 4 * H_M
U_MAIN = 3 * W_A + 3 * W_M

LANES = 128
V7X_VMEM_BYTES = 64 * 1024 * 1024

TM = 512
TM_FFN = 1024
TM_IN = 1024
TF = 512
TN_IN = 1024
ATT_RB = 4
ATT_WIN = 12
ATT_HEADS = 4
ML_CHUNK = 256
ML_BATCH = 1
NEG_BIG = -1e30


def _vmem_limit(block_bytes, scratch_bytes):
    est = 2 * block_bytes + scratch_bytes + 12 * 1024 * 1024
    return int(min(est, V7X_VMEM_BYTES - 6 * 1024 * 1024))


def _nbytes(shape, dtype):
    return int(np.prod(shape)) * jnp.dtype(dtype).itemsize


def _rms_rows(x, g):
    ms = jnp.mean(x * x, axis=-1, keepdims=True)
    return x * lax.rsqrt(ms + EPS) * g


def _ffn_kernel(x_ref, g_ref, wgu_ref, wd_ref, o_ref, hn_ref):
    j = pl.program_id(1)

    def half_step(hn):
        ab = jnp.dot(hn, wgu_ref[...], preferred_element_type=F32)
        a = ab[:, :TF]
        b = ab[:, TF:]
        h = (a * jax.nn.sigmoid(a) * (0.5 * b)).astype(BF16)
        return jnp.dot(h, wd_ref[...].astype(BF16), preferred_element_type=F32)

    @pl.when(j == 0)
    def _():
        x = x_ref[...]
        hn = _rms_rows(x, g_ref[...]).astype(BF16)
        hn_ref[...] = hn
        o_ref[...] = x + half_step(hn)

    @pl.when(j > 0)
    def _():
        o_ref[...] += half_step(hn_ref[...])


def _interleave_gate_up(wg, wu):
    D, F = wg.shape
    nf = F // TF
    return jnp.concatenate([wg.reshape(D, nf, TF), wu.reshape(D, nf, TF)], axis=2).reshape(D, 2 * F)


def _ffn(x, g, wgu, wd):
    T, D = x.shape
    F = wd.shape[0]
    tm = TM_FFN
    blocks = (_nbytes((tm, D), F32) * 2 + _nbytes((D, 2 * TF), BF16) + _nbytes((TF, D), F32))
    scratch = _nbytes((tm, D), BF16)
    return pl.pallas_call(
        _ffn_kernel,
        grid=(T // tm, F // TF),
        in_specs=[
            pl.BlockSpec((tm, D), lambda i, j: (i, 0)),
            pl.BlockSpec((1, D), lambda i, j: (0, 0)),
            pl.BlockSpec((D, 2 * TF), lambda i, j: (0, j)),
            pl.BlockSpec((TF, D), lambda i, j: (j, 0)),
        ],
        out_specs=pl.BlockSpec((tm, D), lambda i, j: (i, 0)),
        out_shape=jax.ShapeDtypeStruct((T, D), F32),
        scratch_shapes=[pltpu.VMEM((tm, D), BF16)],
        compiler_params=pltpu.CompilerParams(
            dimension_semantics=("arbitrary", "arbitrary"),
            vmem_limit_bytes=_vmem_limit(blocks, scratch)),
        name="ffn",
    )(x, g.reshape(1, D), wgu, wd)


def _inproj_kernel(x_ref, g_ref, w_ref, wgate_ref, wkt_ref, gq_ref, gk_ref,
                   u_ref, gates_ref, kt_ref, hn_ref, *, n_u_tiles):
    j = pl.program_id(1)

    n_q_tiles = W_A // TN_IN
    n_qk_tiles = 2 * n_q_tiles

    def project(hn):
        return jnp.dot(hn, w_ref[...], preferred_element_type=F32)

    def store_qk_normed(acc, gain):
        for hh in range(TN_IN // DH_A):
            sl = slice(hh * DH_A, (hh + 1) * DH_A)
            u_ref[:, sl] = _rms_rows(acc[:, sl], gain).astype(BF16)

    @pl.when(j == 0)
    def _():
        hn = _rms_rows(x_ref[...], g_ref[...]).astype(BF16)
        hn_ref[...] = hn
        gates_ref[...] = jnp.dot(hn, wgate_ref[...], preferred_element_type=F32)
        store_qk_normed(project(hn), gq_ref[...] * (DH_A ** -0.5))

    @pl.when((j > 0) & (j < n_qk_tiles))
    def _():
        gain = jnp.where(j < n_q_tiles, gq_ref[...] * (DH_A ** -0.5), gk_ref[...])
        store_qk_normed(project(hn_ref[...]), gain)

    @pl.when((j >= n_qk_tiles) & (j < n_u_tiles))
    def _():
        u_ref[...] = project(hn_ref[...]).astype(BF16)

    @pl.when(j == n_u_tiles)
    def _():
        kt_ref[...] = lax.dot_general(wkt_ref[...], hn_ref[...], (((1,), (1,)), ((), ())),
                                      preferred_element_type=F32).astype(BF16)


def _inproj(x, g, w_main, w_gate, w_kt, g_qn, g_kn):
    T, D = x.shape
    tm = TM_IN
    nu = U_MAIN // TN_IN
    blocks = (_nbytes((tm, D), F32) + _nbytes((D, TN_IN), BF16) + _nbytes((D, LANES), BF16)
              + _nbytes((W_M, D), BF16) + _nbytes((tm, TN_IN), BF16) + _nbytes((tm, LANES), F32)
              + _nbytes((W_M, tm), BF16))
    scratch = _nbytes((tm, D), BF16)
    last = nu - 1
    km_block = (3 * W_A + W_M) // TN_IN

    def w_col_block(j):
        return jnp.minimum(jnp.where(j < km_block, j, j + 1), nu)

    return pl.pallas_call(
        functools.partial(_inproj_kernel, n_u_tiles=nu),
        grid=(T // tm, nu + 1),
        in_specs=[
            pl.BlockSpec((tm, D), lambda i, j: (i, 0)),
            pl.BlockSpec((1, D), lambda i, j: (0, 0)),
            pl.BlockSpec((D, TN_IN), lambda i, j: (0, w_col_block(j))),
            pl.BlockSpec((D, LANES), lambda i, j: (0, 0)),
            pl.BlockSpec((W_M, D), lambda i, j: (0, 0)),
            pl.BlockSpec((1, DH_A), lambda i, j: (0, 0)),
            pl.BlockSpec((1, DH_A), lambda i, j: (0, 0)),
        ],
        out_specs=[
            pl.BlockSpec((tm, TN_IN), lambda i, j: (i, jnp.minimum(j, last))),
            pl.BlockSpec((tm, LANES), lambda i, j: (i, 0)),
            pl.BlockSpec((W_M, tm), lambda i, j: (0, i)),
        ],
        out_shape=[
            jax.ShapeDtypeStruct((T, U_MAIN), BF16),
            jax.ShapeDtypeStruct((T, LANES), F32),
            jax.ShapeDtypeStruct((W_M, T), BF16),
        ],
        scratch_shapes=[pltpu.VMEM((tm, D), BF16)],
        compiler_params=pltpu.CompilerParams(
            dimension_semantics=("arbitrary", "arbitrary"),
            vmem_limit_bytes=_vmem_limit(blocks, scratch)),
        name="inproj",
    )(x, g.reshape(1, D), w_main, w_gate, w_kt, g_qn.reshape(1, DH_A), g_kn.reshape(1, DH_A))


def _attn_bias_table(rpb):
    a = np.arange(ATT_RB)[:, None]
    ik = np.arange(ATT_WIN)[None, :]
    kh = WIN_H
    valid0 = ik < kh
    dr0 = ik - a + (WIN_H - 1)
    valid1 = (ik >= a) & (ik < a + kh)
    dr1 = ik - a + (WIN_H - 1) - kh // 2
    valid2 = (ik >= ATT_WIN - kh) & (ik < ATT_WIN)
    dr2 = (ik - ATT_WIN) - (a - ATT_RB) + (WIN_H - 1)
    valid = np.stack([np.broadcast_to(valid0, dr0.shape), valid1,
                      np.broadcast_to(valid2, dr0.shape)])
    dr = np.stack([dr0, dr1, dr2])

    c = np.arange(GRID_W)
    cs = np.clip(c - WIN_W // 2, 0, GRID_W - WIN_W)
    colmask = (c[None, :] >= cs[:, None]) & (c[None, :] < cs[:, None] + WIN_W)
    dc = np.clip(c[None, :] - c[:, None], -(WIN_W - 1), WIN_W - 1) + (WIN_W - 1)

    onehot = (dc[None] == np.arange(2 * WIN_W - 1)[:, None, None]).astype(np.float32)
    tiles = jnp.einsum("hrc,cqk->hrqk", rpb.astype(F32), jnp.asarray(onehot),
                       precision=lax.Precision.HIGHEST)
    tiles = jnp.where(jnp.asarray(colmask), tiles, NEG_BIG)
    masked = jnp.full((H_A, GRID_W, GRID_W), NEG_BIG, F32)
    classes = []
    for cls in range(3):
        rows = []
        for qa in range(ATT_RB):
            rows.append(jnp.concatenate(
                [tiles[:, int(dr[cls, qa, kk])] if valid[cls, qa, kk] else masked
                 for kk in range(ATT_WIN)], axis=-1))
        classes.append(jnp.concatenate(rows, axis=1))
    return jnp.stack(classes)


def _attn_kernel(q_ref, k_ref, v_ref, bias_ref, o_ref, *, n_rows):
    i = pl.program_id(2)
    nb = n_rows // ATT_RB
    cls = jnp.where(i == 0, 0, jnp.where(i == nb - 1, 2, 1))

    for hh in range(ATT_HEADS):
        sl = slice(hh * DH_A, (hh + 1) * DH_A)
        q = q_ref[0, :, sl]
        kw = k_ref[0, :, sl]
        vw = v_ref[0, :, sl]
        s = lax.dot_general(q, kw, (((1,), (1,)), ((), ())), preferred_element_type=F32)
        s = s + bias_ref[cls, hh]
        m = jnp.max(s, axis=-1, keepdims=True)
        p = jnp.exp(s - m)
        l = jnp.sum(p, axis=-1, keepdims=True)
        o = jnp.dot(p.astype(BF16), vw, preferred_element_type=F32)
        o_ref[0, :, sl] = (o / l).astype(BF16)


def _attention(u, bias, B, S):
    R = S // GRID_W
    assert R % ATT_RB == 0 and R >= ATT_WIN and R // ATT_RB >= 3
    tq = ATT_RB * GRID_W
    tk = ATT_WIN * GRID_W
    hw = ATT_HEADS * DH_A
    ng = H_A // ATT_HEADS
    blocks = (_nbytes((tq, hw), BF16) * 2 + _nbytes((tk, hw), BF16) * 2
              + _nbytes((3, ATT_HEADS, tq, tk), F32))

    def win_start(i):
        return jnp.clip(i * ATT_RB - WIN_H // 2, 0, R - ATT_WIN) * GRID_W

    return pl.pallas_call(
        functools.partial(_attn_kernel, n_rows=R),
        grid=(ng, B, R // ATT_RB),
        in_specs=[
            pl.BlockSpec((1, tq, hw), lambda h, b, i: (b, i, h)),
            pl.BlockSpec((pl.Element(1), pl.Element(tk), pl.Element(hw)),
                         lambda h, b, i: (b, win_start(i), (ng + h) * hw)),
            pl.BlockSpec((pl.Element(1), pl.Element(tk), pl.Element(hw)),
                         lambda h, b, i: (b, win_start(i), (2 * ng + h) * hw)),
            pl.BlockSpec((3, ATT_HEADS, tq, tk), lambda h, b, i: (0, h, 0, 0)),
        ],
        out_specs=pl.BlockSpec((1, tq, hw), lambda h, b, i: (b, i, h)),
        out_shape=jax.ShapeDtypeStruct((B, S, W_A), BF16),
        compiler_params=pltpu.CompilerParams(
            dimension_semantics=("arbitrary", "arbitrary", "arbitrary"),
            vmem_limit_bytes=_vmem_limit(blocks, 0)),
        name="attention",
    )(u, u, u, bias)


def _mlstm_kernel(*refs, reverse, n_chunks):
    bi_ref, bf_ref, q_ref = refs[:3]
    kt_refs = refs[3:3 + ML_BATCH]
    rest = refs[3 + ML_BATCH:]
    if reverse:
        (v_ref, gi_ref, gf_ref, hf_ref, og_ref, gmh_ref,
         o_ref, ct_ref, m_ref, b_ref, e_ref) = rest
    else:
        (v_ref, gi_ref, gf_ref, o_ref, ct_ref, m_ref, b_ref, e_ref) = rest
    L = ML_CHUNK
    c = pl.program_id(1)
    d = 1 if reverse else 0

    row = lax.broadcasted_iota(jnp.int32, (L, L), 0)
    col = lax.broadcasted_iota(jnp.int32, (L, L), 1)
    tri = (col >= row) if reverse else (col <= row)
    eye = row == col

    @pl.when(c == 0)
    def _():
        ct_ref[...] = jnp.zeros_like(ct_ref)
        m_ref[...] = jnp.zeros_like(m_ref)
        cum = jnp.where((row >= col) if reverse else (row <= col), 1.0, 0.0).astype(F32)
        for bb in range(ML_BATCH):
            for hh in range(H_M):
                ig = gi_ref[bb, hh] + bi_ref[d, hh]
                lf = jax.nn.log_sigmoid(gf_ref[bb, hh] + bf_ref[d, hh])
                b = jnp.dot(lf, cum, preferred_element_type=F32, precision=lax.Precision.HIGHEST)
                b_ref[bb * H_M + hh] = b
                e_ref[bb * H_M + hh] = ig - b

    cc = (n_chunks - 1 - c) if reverse else c

    for ch in range(ML_BATCH * H_M):
        bb, hh = divmod(ch, H_M)
        sl = slice(hh * DH_M, (hh + 1) * DH_M)
        b_row = b_ref[ch, pl.ds(cc, 1), :]
        e_row = e_ref[ch, pl.ds(cc, 1), :]
        b_col = jnp.sum(jnp.where(eye, b_row, 0.0), axis=1, keepdims=True)
        e_col = jnp.sum(jnp.where(eye, e_row, 0.0), axis=1, keepdims=True)
        a_tot = b_row[:, 0:1] if reverse else b_row[:, L - 1:L]
        m_prev = m_ref[ch]

        q = q_ref[bb, :, sl]
        kt = kt_refs[bb][sl, :]
        v = v_ref[bb, :, sl]

        dmat = jnp.where(tri, b_col + e_row, -jnp.inf)
        inter = b_col + m_prev
        m_t = jnp.maximum(inter, jnp.max(dmat, axis=1, keepdims=True))
        dexp = jnp.exp(dmat - m_t)
        s = jnp.dot(q, kt, preferred_element_type=F32)
        sqk = s * dexp
        sc = jnp.exp(inter - m_t)
        ct = ct_ref[ch]
        inter_aug = jnp.dot(q, ct.astype(BF16), preferred_element_type=F32)
        num = sc * inter_aug[:, :DH_M] + jnp.dot(sqk.astype(BF16), v, preferred_element_type=F32)
        den = sc * inter_aug[:, DH_M:] + jnp.sum(sqk, axis=1, keepdims=True)
        rden = 1.0 / jnp.maximum(jnp.abs(den), jnp.exp(-m_t))
        hdir = num * jnp.concatenate([rden] * (DH_M // LANES), axis=1)

        g_col = a_tot + e_col
        m_loc = jnp.max(g_col, axis=0, keepdims=True)
        m_new = jnp.maximum(a_tot + m_prev, m_loc)
        s_prev = jnp.exp(a_tot + m_prev - m_new)
        w_col = jnp.exp(g_col - m_new)
        vw_aug = jnp.concatenate(
            [(v.astype(F32) * w_col).astype(BF16),
             jnp.broadcast_to(w_col, (L, LANES)).astype(BF16)], axis=1)
        ct_ref[ch] = s_prev * ct + jnp.dot(kt, vw_aug, preferred_element_type=F32)
        m_ref[ch] = m_new

        if reverse:
            hm = _rms_rows(hf_ref[bb, :, sl] + hdir, gmh_ref[:, sl])
            o_ref[bb, :, sl] = (jax.nn.sigmoid(og_ref[bb, :, sl].astype(F32)) * hm).astype(BF16)
        else:
            o_ref[bb, :, sl] = hdir


def _mlstm_sweep(u, kt, gates_t, b_igate, b_fgate, B, S, *, reverse, hf=None, g_mh=None):
    L = ML_CHUNK
    N = S // L
    d = 1 if reverse else 0
    qoff = 3 * W_A // W_M
    cidx = (lambda c: N - 1 - c) if reverse else (lambda c: c)
    smem = pl.BlockSpec(memory_space=pltpu.SMEM)
    nb = ML_BATCH
    assert B % nb == 0
    nch = nb * H_M
    kt_specs = [pl.BlockSpec((W_M, L), functools.partial(
        lambda b, c, r: (0, (nb * b + r) * N + cidx(c)), r=r)) for r in range(nb)]
    in_specs = [
        smem, smem,
        pl.BlockSpec((nb, L, W_M), lambda b, c: (b, cidx(c), qoff)),
        *kt_specs,
        pl.BlockSpec((nb, L, W_M), lambda b, c: (b, cidx(c), qoff + 1)),
        pl.BlockSpec((nb, H_M, N, L), lambda b, c: (b, d, 0, 0)),
        pl.BlockSpec((nb, H_M, N, L), lambda b, c: (b, 2 + d, 0, 0)),
    ]
    args = [b_igate, b_fgate, u, *([kt] * nb), u, gates_t, gates_t]
    if reverse:
        in_specs += [
            pl.BlockSpec((nb, L, W_M), lambda b, c: (b, cidx(c), 0)),
            pl.BlockSpec((nb, L, W_M), lambda b, c: (b, cidx(c), qoff + 2)),
            pl.BlockSpec((1, W_M), lambda b, c: (0, 0)),
        ]
        args += [hf, u, g_mh.reshape(1, W_M)]
        out_dtype = BF16
    else:
        out_dtype = F32
    blocks = nb * (_nbytes((L, W_M), BF16) * 4 + _nbytes((H_M, N, L), F32) * 2
                   + _nbytes((L, W_M), F32) * 2)
    scratch = (nch * _nbytes((DH_M, DH_M + LANES), F32) + 2 * nch * _nbytes((N, L), F32)
               + nch * 8 * LANES * 4)
    return pl.pallas_call(
        functools.partial(_mlstm_kernel, reverse=reverse, n_chunks=N),
        grid=(B // nb, N),
        in_specs=in_specs,
        out_specs=pl.BlockSpec((nb, L, W_M), lambda b, c: (b, cidx(c), 0)),
        out_shape=jax.ShapeDtypeStruct((B, S, W_M), out_dtype),
        scratch_shapes=[
            pltpu.VMEM((nch, DH_M, DH_M + LANES), F32),
            pltpu.VMEM((nch, 1, 1), F32),
            pltpu.VMEM((nch, N, L), F32),
            pltpu.VMEM((nch, N, L), F32),
        ],
        compiler_params=pltpu.CompilerParams(
            dimension_semantics=("arbitrary", "arbitrary"),
            vmem_limit_bytes=_vmem_limit(blocks, scratch)),
        name="mlstm_bwd" if reverse else "mlstm_fwd",
    )(*args)


def _outproj_kernel(x_ref, ya_ref, ym_ref, wa_ref, wm_ref, o_ref):
    o_ref[...] = (x_ref[...]
                  + jnp.dot(ya_ref[...], wa_ref[...], preferred_element_type=F32)
                  + jnp.dot(ym_ref[...], wm_ref[...], preferred_element_type=F32))


def _outproj(x, ya, ym, w_a, w_m):
    T, D = x.shape
    blocks = (_nbytes((TM, D), F32) * 2 + _nbytes((TM, W_A), BF16) * 2 + _nbytes((W_A, D), BF16) * 2)
    return pl.pallas_call(
        _outproj_kernel,
        grid=(T // TM,),
        in_specs=[
            pl.BlockSpec((TM, D), lambda i: (i, 0)),
            pl.BlockSpec((TM, W_A), lambda i: (i, 0)),
            pl.BlockSpec((TM, W_M), lambda i: (i, 0)),
            pl.BlockSpec((W_A, D), lambda i: (0, 0)),
            pl.BlockSpec((W_M, D), lambda i: (0, 0)),
        ],
        out_specs=pl.BlockSpec((TM, D), lambda i: (i, 0)),
        out_shape=jax.ShapeDtypeStruct((T, D), F32),
        compiler_params=pltpu.CompilerParams(
            dimension_semantics=("arbitrary",),
            vmem_limit_bytes=_vmem_limit(blocks, 0)),
        name="outproj",
    )(x, ya, ym, w_a, w_m)


def _ple_kernel(x_ref, g_ref, pe_ref, wg_ref, wp_ref, o_ref):
    x = x_ref[...]
    hn = _rms_rows(x, g_ref[...]).astype(BF16)
    gate = jax.nn.sigmoid(jnp.dot(hn, wg_ref[...], preferred_element_type=F32))
    proj = jnp.dot(pe_ref[...].astype(BF16), wp_ref[...], preferred_element_type=F32)
    o_ref[...] = x + gate * proj


def _ple(x, g, pe, w_gate, w_proj):
    T, D = x.shape
    P = pe.shape[1]
    blocks = (_nbytes((TM, D), F32) * 2 + _nbytes((TM, P), F32) + _nbytes((D, D), BF16)
              + _nbytes((P, D), BF16))
    return pl.pallas_call(
        _ple_kernel,
        grid=(T // TM,),
        in_specs=[
            pl.BlockSpec((TM, D), lambda i: (i, 0)),
            pl.BlockSpec((1, D), lambda i: (0, 0)),
            pl.BlockSpec((TM, P), lambda i: (i, 0)),
            pl.BlockSpec((D, D), lambda i: (0, 0)),
            pl.BlockSpec((P, D), lambda i: (0, 0)),
        ],
        out_specs=pl.BlockSpec((TM, D), lambda i: (i, 0)),
        out_shape=jax.ShapeDtypeStruct((T, D), F32),
        compiler_params=pltpu.CompilerParams(
            dimension_semantics=("arbitrary",),
            vmem_limit_bytes=_vmem_limit(blocks, 0)),
        name="ple",
    )(x, g.reshape(1, D), pe, w_gate, w_proj)


def _layer(x, pe, w):
    B, S, D = x.shape
    T = B * S
    x2 = x.reshape(T, D)
    x2 = _ffn(x2, w["g_ffn1"], w["w1gu"], w["w1d"])

    u, gates, kt = _inproj(x2, w["g_mix"], w["w_in_main"], w["w_in_gate"], w["w_in_kt"],
                           w["g_qn"], w["g_kn"])
    u = u.reshape(B, S, U_MAIN)
    ya = _attention(u, w["attn_bias"], B, S)

    n_chunks = S // ML_CHUNK
    gates_t = jnp.transpose(gates[:, :N_GATES].reshape(B, S, N_GATES), (0, 2, 1))
    gates_t = gates_t.reshape(B, N_GATES, n_chunks, ML_CHUNK)
    hf = _mlstm_sweep(u, kt, gates_t, w["b_igate"], w["b_fgate"], B, S, reverse=False)
    ym = _mlstm_sweep(u, kt, gates_t, w["b_igate"], w["b_fgate"], B, S, reverse=True,
                      hf=hf, g_mh=w["g_mh"])

    x2 = _outproj(x2, ya.reshape(T, W_A), ym.reshape(T, W_M), w["w_out_a"], w["w_out_m"])
    x2 = _ffn(x2, w["g_ffn2"], w["w2gu"], w["w2d"])
    x2 = _ple(x2, w["g_ple"], pe.reshape(T, -1), w["w_ple_gate"], w["w_ple_proj"])
    return x2.reshape(B, S, D)


def kernel(x_prompt, x_sample, p_prompt, p_sample, g_ffn1, w_ffn1_gate, w_ffn1_up, w_ffn1_down, g_mix, w_in, b_igate, b_fgate, g_qn, g_kn, rpb, g_mh, w_out, g_ffn2, w_ffn2_gate, w_ffn2_up, w_ffn2_down, g_ple, w_ple_gate, w_ple_proj):
    depth = g_ffn1.shape[0]
    xs = [x_prompt, x_sample]
    ps = [p_prompt, p_sample]
    for i in range(depth):
        w_in_i = w_in[i]
        k_lo = 3 * W_A + W_M
        k_hi = k_lo + W_M
        u_end = 3 * W_A + 4 * W_M
        gate_cols = jnp.pad(w_in_i[:, u_end:], ((0, 0), (0, LANES - N_GATES)))
        w = {
            "g_ffn1": g_ffn1[i],
            "w1gu": _interleave_gate_up(w_ffn1_gate[i], w_ffn1_up[i]).astype(BF16),
            "w1d": w_ffn1_down[i],
            "g_mix": g_mix[i],
            "w_in_main": w_in_i.astype(BF16),
            "w_in_gate": gate_cols.astype(BF16),
            "w_in_kt": (w_in_i[:, k_lo:k_hi] * (DH_M ** -0.5)).T.astype(BF16),
            "b_igate": b_igate[i], "b_fgate": b_fgate[i], "g_qn": g_qn[i], "g_kn": g_kn[i],
            "attn_bias": _attn_bias_table(rpb[i]), "g_mh": g_mh[i],
            "w_out_a": w_out[i, :W_A].astype(BF16), "w_out_m": w_out[i, W_A:].astype(BF16),
            "g_ffn2": g_ffn2[i],
            "w2gu": _interleave_gate_up(w_ffn2_gate[i], w_ffn2_up[i]).astype(BF16),
            "w2d": w_ffn2_down[i],
            "g_ple": g_ple[i], "w_ple_gate": w_ple_gate[i].astype(BF16),
            "w_ple_proj": w_ple_proj[i].astype(BF16),
        }
        xs = [_layer(x, p[i], w) for x, p in zip(xs, ps)]
    return (xs[0], xs[1])
```

```python
import functools

import numpy as np
import jax
import jax.numpy as jnp
from jax import lax
from jax.experimental import pallas as pl
from jax.experimental.pallas import tpu as pltpu

F32 = jnp.float32
BF16 = jnp.bfloat16

GRID_W = 64
WIN_H = 8
WIN_W = 16
H_A = 8
DH_A = 128
H_M = 4
DH_M = 256
W_A = H_A * DH_A
W_M = H_M * DH_M
EPS = 1e-6
N_GATES = 4 * H_M
U_MAIN = 3 * W_A + 3 * W_M

LANES = 128
V7X_VMEM_BYTES = 64 * 1024 * 1024

TM = 512
TM_FFN = 1024
TM_IN = 1024
TF = 512
TN_IN = 1024
ATT_RB = 4
ATT_WIN = 12
ATT_HEADS = 4
ML_CHUNK = 256
ML_BATCH = 1
NEG_BIG = -1e30


def _vmem_limit(block_bytes, scratch_bytes):
    est = 2 * block_bytes + scratch_bytes + 12 * 1024 * 1024
    return int(min(est, V7X_VMEM_BYTES - 6 * 1024 * 1024))


def _nbytes(shape, dtype):
    return int(np.prod(shape)) * jnp.dtype(dtype).itemsize


def _rms_rows(x, g):
    ms = jnp.mean(x * x, axis=-1, keepdims=True)
    return x * lax.rsqrt(ms + EPS) * g


def _ffn_kernel(x_ref, g_ref, wg_ref, wu_ref, wd_ref, o_ref, hn_ref):
    j = pl.program_id(1)

    def half_step(hn):
        a = jnp.dot(hn, wg_ref[...], preferred_element_type=F32)
        b = jnp.dot(hn, wu_ref[...], preferred_element_type=F32)
        h = (a * jax.nn.sigmoid(a) * (0.5 * b)).astype(BF16)
        return jnp.dot(h, wd_ref[...].astype(BF16), preferred_element_type=F32)

    @pl.when(j == 0)
    def _():
        x = x_ref[...]
        hn = _rms_rows(x, g_ref[...]).astype(BF16)
        hn_ref[...] = hn
        o_ref[...] = x + half_step(hn)

    @pl.when(j > 0)
    def _():
        o_ref[...] += half_step(hn_ref[...])


def _ffn(x, g, wg, wu, wd):
    T, D = x.shape
    F = wd.shape[0]
    tm = TM_FFN
    blocks = (_nbytes((tm, D), F32) * 2 + _nbytes((D, 2 * TF), BF16) + _nbytes((TF, D), F32))
    scratch = _nbytes((tm, D), BF16)
    return pl.pallas_call(
        _ffn_kernel,
        grid=(T // tm, F // TF),
        in_specs=[
            pl.BlockSpec((tm, D), lambda i, j: (i, 0)),
            pl.BlockSpec((1, D), lambda i, j: (0, 0)),
            pl.BlockSpec((D, TF), lambda i, j: (0, j)),
            pl.BlockSpec((D, TF), lambda i, j: (0, j)),
            pl.BlockSpec((TF, D), lambda i, j: (j, 0)),
        ],
        out_specs=pl.BlockSpec((tm, D), lambda i, j: (i, 0)),
        out_shape=jax.ShapeDtypeStruct((T, D), F32),
        scratch_shapes=[pltpu.VMEM((tm, D), BF16)],
        compiler_params=pltpu.CompilerParams(
            dimension_semantics=("arbitrary", "arbitrary"),
            vmem_limit_bytes=_vmem_limit(blocks, scratch)),
        name="ffn",
    )(x, g.reshape(1, D), wg, wu, wd)


def _inproj_kernel(x_ref, g_ref, w_ref, wgate_ref, wkt_ref, gq_ref, gk_ref,
                   u_ref, gates_ref, kt_ref, hn_ref, *, n_u_tiles):
    j = pl.program_id(1)

    n_q_tiles = W_A // TN_IN
    n_qk_tiles = 2 * n_q_tiles

    def project(hn):
        return jnp.dot(hn, w_ref[...], preferred_element_type=F32)

    def store_qk_normed(acc, gain):
        for hh in range(TN_IN // DH_A):
            sl = slice(hh * DH_A, (hh + 1) * DH_A)
            u_ref[:, sl] = _rms_rows(acc[:, sl], gain).astype(BF16)

    @pl.when(j == 0)
    def _():
        hn = _rms_rows(x_ref[...], g_ref[...]).astype(BF16)
        hn_ref[...] = hn
        gates_ref[...] = jnp.dot(hn, wgate_ref[...], preferred_element_type=F32)
        store_qk_normed(project(hn), gq_ref[...] * (DH_A ** -0.5))

    @pl.when((j > 0) & (j < n_qk_tiles))
    def _():
        gain = jnp.where(j < n_q_tiles, gq_ref[...] * (DH_A ** -0.5), gk_ref[...])
        store_qk_normed(project(hn_ref[...]), gain)

    @pl.when((j >= n_qk_tiles) & (j < n_u_tiles))
    def _():
        u_ref[...] = project(hn_ref[...]).astype(BF16)

    @pl.when(j == n_u_tiles)
    def _():
        kt_ref[...] = lax.dot_general(wkt_ref[...], hn_ref[...], (((1,), (1,)), ((), ())),
                                      preferred_element_type=F32).astype(BF16)


def _inproj(x, g, w_main, w_gate, w_kt, g_qn, g_kn):
    T, D = x.shape
    tm = TM_IN
    nu = U_MAIN // TN_IN
    blocks = (_nbytes((tm, D), F32) + _nbytes((D, TN_IN), BF16) + _nbytes((D, LANES), BF16)
              + _nbytes((W_M, D), BF16) + _nbytes((tm, TN_IN), BF16) + _nbytes((tm, LANES), F32)
              + _nbytes((W_M, tm), BF16))
    scratch = _nbytes((tm, D), BF16)
    last = nu - 1
    km_block = (3 * W_A + W_M) // TN_IN

    def w_col_block(j):
        return jnp.minimum(jnp.where(j < km_block, j, j + 1), nu)

    return pl.pallas_call(
        functools.partial(_inproj_kernel, n_u_tiles=nu),
        grid=(T // tm, nu + 1),
        in_specs=[
            pl.BlockSpec((tm, D), lambda i, j: (i, 0)),
            pl.BlockSpec((1, D), lambda i, j: (0, 0)),
            pl.BlockSpec((D, TN_IN), lambda i, j: (0, w_col_block(j))),
            pl.BlockSpec((D, LANES), lambda i, j: (0, 0)),
            pl.BlockSpec((W_M, D), lambda i, j: (0, 0)),
            pl.BlockSpec((1, DH_A), lambda i, j: (0, 0)),
            pl.BlockSpec((1, DH_A), lambda i, j: (0, 0)),
        ],
        out_specs=[
            pl.BlockSpec((tm, TN_IN), lambda i, j: (i, jnp.minimum(j, last))),
            pl.BlockSpec((tm, LANES), lambda i, j: (i, 0)),
            pl.BlockSpec((W_M, tm), lambda i, j: (0, i)),
        ],
        out_shape=[
            jax.ShapeDtypeStruct((T, U_MAIN), BF16),
            jax.ShapeDtypeStruct((T, LANES), F32),
            jax.ShapeDtypeStruct((W_M, T), BF16),
        ],
        scratch_shapes=[pltpu.VMEM((tm, D), BF16)],
        compiler_params=pltpu.CompilerParams(
            dimension_semantics=("arbitrary", "arbitrary"),
            vmem_limit_bytes=_vmem_limit(blocks, scratch)),
        name="inproj",
    )(x, g.reshape(1, D), w_main, w_gate, w_kt, g_qn.reshape(1, DH_A), g_kn.reshape(1, DH_A))


def _attn_bias_table(rpb):
    a = np.arange(ATT_RB)[:, None]
    ik = np.arange(ATT_WIN)[None, :]
    kh = WIN_H
    valid0 = ik < kh
    dr0 = ik - a + (WIN_H - 1)
    valid1 = (ik >= a) & (ik < a + kh)
    dr1 = ik - a + (WIN_H - 1) - kh // 2
    valid2 = (ik >= ATT_WIN - kh) & (ik < ATT_WIN)
    dr2 = (ik - ATT_WIN) - (a - ATT_RB) + (WIN_H - 1)
    valid = np.stack([np.broadcast_to(valid0, dr0.shape), valid1,
                      np.broadcast_to(valid2, dr0.shape)])
    dr = np.stack([dr0, dr1, dr2])

    c = np.arange(GRID_W)
    cs = np.clip(c - WIN_W // 2, 0, GRID_W - WIN_W)
    colmask = (c[None, :] >= cs[:, None]) & (c[None, :] < cs[:, None] + WIN_W)
    dc = np.clip(c[None, :] - c[:, None], -(WIN_W - 1), WIN_W - 1) + (WIN_W - 1)

    onehot = (dc[None] == np.arange(2 * WIN_W - 1)[:, None, None]).astype(np.float32)
    tiles = jnp.einsum("hrc,cqk->hqrk", rpb.astype(F32), jnp.asarray(onehot),
                       precision=lax.Precision.HIGHEST)
    tiles = jnp.where(jnp.asarray(colmask)[:, None, :], tiles, NEG_BIG)
    seq = tiles.reshape(H_A, GRID_W, (2 * WIN_H - 1) * GRID_W)
    classes = []
    for cls in range(3):
        rows = []
        for qa in range(ATT_RB):
            kk = np.nonzero(valid[cls, qa])[0]
            k0, k1 = int(kk[0]), int(kk[-1]) + 1
            d0 = int(dr[cls, qa, k0])
            assert np.array_equal(kk, np.arange(k0, k1))
            assert np.array_equal(dr[cls, qa, k0:k1], np.arange(d0, d0 + k1 - k0))
            strip = seq[:, :, d0 * GRID_W:(d0 + k1 - k0) * GRID_W]
            rows.append(jnp.pad(strip, ((0, 0), (0, 0), (k0 * GRID_W, (ATT_WIN - k1) * GRID_W)),
                                constant_values=NEG_BIG))
        classes.append(jnp.concatenate(rows, axis=1))
    return jnp.stack(classes)


def _attn_kernel(q_ref, k_ref, v_ref, bias_ref, o_ref, *, n_rows):
    i = pl.program_id(2)
    nb = n_rows // ATT_RB
    cls = jnp.where(i == 0, 0, jnp.where(i == nb - 1, 2, 1))

    for hh in range(ATT_HEADS):
        sl = slice(hh * DH_A, (hh + 1) * DH_A)
        q = q_ref[0, :, sl]
        kw = k_ref[0, :, sl]
        vw = v_ref[0, :, sl]
        s = lax.dot_general(q, kw, (((1,), (1,)), ((), ())), preferred_element_type=F32)
        s = s + bias_ref[cls, hh]
        m = jnp.max(s, axis=-1, keepdims=True)
        p = jnp.exp(s - m)
        l = jnp.sum(p, axis=-1, keepdims=True)
        o = jnp.dot(p.astype(BF16), vw, preferred_element_type=F32)
        o_ref[0, :, sl] = (o / l).astype(BF16)


def _attention(u, bias, B, S):
    R = S // GRID_W
    assert R % ATT_RB == 0 and R >= ATT_WIN and R // ATT_RB >= 3
    tq = ATT_RB * GRID_W
    tk = ATT_WIN * GRID_W
    hw = ATT_HEADS * DH_A
    ng = H_A // ATT_HEADS
    blocks = (_nbytes((tq, hw), BF16) * 2 + _nbytes((tk, hw), BF16) * 2
              + _nbytes((3, ATT_HEADS, tq, tk), F32))

    def win_start(i):
        return jnp.clip(i * ATT_RB - WIN_H // 2, 0, R - ATT_WIN) * GRID_W

    return pl.pallas_call(
        functools.partial(_attn_kernel, n_rows=R),
        grid=(ng, B, R // ATT_RB),
        in_specs=[
            pl.BlockSpec((1, tq, hw), lambda h, b, i: (b, i, h)),
            pl.BlockSpec((pl.Element(1), pl.Element(tk), pl.Element(hw)),
                         lambda h, b, i: (b, win_start(i), (ng + h) * hw)),
            pl.BlockSpec((pl.Element(1), pl.Element(tk), pl.Element(hw)),
                         lambda h, b, i: (b, win_start(i), (2 * ng + h) * hw)),
            pl.BlockSpec((3, ATT_HEADS, tq, tk), lambda h, b, i: (0, h, 0, 0)),
        ],
        out_specs=pl.BlockSpec((1, tq, hw), lambda h, b, i: (b, i, h)),
        out_shape=jax.ShapeDtypeStruct((B, S, W_A), BF16),
        compiler_params=pltpu.CompilerParams(
            dimension_semantics=("arbitrary", "arbitrary", "arbitrary"),
            vmem_limit_bytes=_vmem_limit(blocks, 0)),
        name="attention",
    )(u, u, u, bias)


def _mlstm_kernel(*refs, reverse, n_chunks):
    bi_ref, bf_ref, q_ref = refs[:3]
    kt_refs = refs[3:3 + ML_BATCH]
    rest = refs[3 + ML_BATCH:]
    if reverse:
        (v_ref, gi_ref, gf_ref, hf_ref, og_ref, gmh_ref,
         o_ref, ct_ref, m_ref, b_ref, e_ref) = rest
    else:
        (v_ref, gi_ref, gf_ref, o_ref, ct_ref, m_ref, b_ref, e_ref) = rest
    L = ML_CHUNK
    c = pl.program_id(1)
    d = 1 if reverse else 0

    row = lax.broadcasted_iota(jnp.int32, (L, L), 0)
    col = lax.broadcasted_iota(jnp.int32, (L, L), 1)
    tri = (col >= row) if reverse else (col <= row)
    eye = row == col

    @pl.when(c == 0)
    def _():
        ct_ref[...] = jnp.zeros_like(ct_ref)
        m_ref[...] = jnp.zeros_like(m_ref)
        cum = jnp.where((row >= col) if reverse else (row <= col), 1.0, 0.0).astype(F32)
        for bb in range(ML_BATCH):
            for hh in range(H_M):
                ig = gi_ref[bb, hh] + bi_ref[d, hh]
                lf = jax.nn.log_sigmoid(gf_ref[bb, hh] + bf_ref[d, hh])
                b = jnp.dot(lf, cum, preferred_element_type=F32, precision=lax.Precision.HIGHEST)
                b_ref[bb * H_M + hh] = b
                e_ref[bb * H_M + hh] = ig - b

    cc = (n_chunks - 1 - c) if reverse else c

    for ch in range(ML_BATCH * H_M):
        bb, hh = divmod(ch, H_M)
        sl = slice(hh * DH_M, (hh + 1) * DH_M)
        b_row = b_ref[ch, pl.ds(cc, 1), :]
        e_row = e_ref[ch, pl.ds(cc, 1), :]
        b_col = jnp.sum(jnp.where(eye, b_row, 0.0), axis=1, keepdims=True)
        e_col = jnp.sum(jnp.where(eye, e_row, 0.0), axis=1, keepdims=True)
        a_tot = b_row[:, 0:1] if reverse else b_row[:, L - 1:L]
        m_prev = m_ref[ch]

        q = q_ref[bb, :, sl]
        kt = kt_refs[bb][sl, :]
        v = v_ref[bb, :, sl]

        dmat = jnp.where(tri, b_col + e_row, -jnp.inf)
        inter = b_col + m_prev
        m_t = jnp.maximum(inter, jnp.max(dmat, axis=1, keepdims=True))
        dexp = jnp.exp(dmat - m_t)
        s = jnp.dot(q, kt, preferred_element_type=F32)
        sqk = s * dexp
        sc = jnp.exp(inter - m_t)
        ct = ct_ref[ch]
        inter_aug = jnp.dot(q, ct.astype(BF16), preferred_element_type=F32)
        num = sc * inter_aug[:, :DH_M] + jnp.dot(sqk.astype(BF16), v, preferred_element_type=F32)
        den = sc * inter_aug[:, DH_M:] + jnp.sum(sqk, axis=1, keepdims=True)
        rden = 1.0 / jnp.maximum(jnp.abs(den), jnp.exp(-m_t))
        hdir = num * jnp.concatenate([rden] * (DH_M // LANES), axis=1)

        g_col = a_tot + e_col
        m_loc = jnp.max(g_col, axis=0, keepdims=True)
        m_new = jnp.maximum(a_tot + m_prev, m_loc)
        s_prev = jnp.exp(a_tot + m_prev - m_new)
        w_col = jnp.exp(g_col - m_new)
        vw_aug = jnp.concatenate(
            [(v.astype(F32) * w_col).astype(BF16),
             jnp.broadcast_to(w_col, (L, LANES)).astype(BF16)], axis=1)
        ct_ref[ch] = s_prev * ct + jnp.dot(kt, vw_aug, preferred_element_type=F32)
        m_ref[ch] = m_new

        if reverse:
            hm = _rms_rows(hf_ref[bb, :, sl] + hdir, gmh_ref[:, sl])
            o_ref[bb, :, sl] = (jax.nn.sigmoid(og_ref[bb, :, sl].astype(F32)) * hm).astype(BF16)
        else:
            o_ref[bb, :, sl] = hdir


def _mlstm_sweep(u, kt, gates_t, b_igate, b_fgate, B, S, *, reverse, hf=None, g_mh=None):
    L = ML_CHUNK
    N = S // L
    d = 1 if reverse else 0
    qoff = 3 * W_A // W_M
    cidx = (lambda c: N - 1 - c) if reverse else (lambda c: c)
    smem = pl.BlockSpec(memory_space=pltpu.SMEM)
    nb = ML_BATCH
    assert B % nb == 0
    nch = nb * H_M
    kt_specs = [pl.BlockSpec((W_M, L), functools.partial(
        lambda b, c, r: (0, (nb * b + r) * N + cidx(c)), r=r)) for r in range(nb)]
    in_specs = [
        smem, smem,
        pl.BlockSpec((nb, L, W_M), lambda b, c: (b, cidx(c), qoff)),
        *kt_specs,
        pl.BlockSpec((nb, L, W_M), lambda b, c: (b, cidx(c), qoff + 1)),
        pl.BlockSpec((nb, H_M, N, L), lambda b, c: (b, d, 0, 0)),
        pl.BlockSpec((nb, H_M, N, L), lambda b, c: (b, 2 + d, 0, 0)),
    ]
    args = [b_igate, b_fgate, u, *([kt] * nb), u, gates_t, gates_t]
    if reverse:
        in_specs += [
            pl.BlockSpec((nb, L, W_M), lambda b, c: (b, cidx(c), 0)),
            pl.BlockSpec((nb, L, W_M), lambda b, c: (b, cidx(c), qoff + 2)),
            pl.BlockSpec((1, W_M), lambda b, c: (0, 0)),
        ]
        args += [hf, u, g_mh.reshape(1, W_M)]
        out_dtype = BF16
    else:
        out_dtype = F32
    blocks = nb * (_nbytes((L, W_M), BF16) * 4 + _nbytes((H_M, N, L), F32) * 2
                   + _nbytes((L, W_M), F32) * 2)
    scratch = (nch * _nbytes((DH_M, DH_M + LANES), F32) + 2 * nch * _nbytes((N, L), F32)
               + nch * 8 * LANES * 4)
    return pl.pallas_call(
        functools.partial(_mlstm_kernel, reverse=reverse, n_chunks=N),
        grid=(B // nb, N),
        in_specs=in_specs,
        out_specs=pl.BlockSpec((nb, L, W_M), lambda b, c: (b, cidx(c), 0)),
        out_shape=jax.ShapeDtypeStruct((B, S, W_M), out_dtype),
        scratch_shapes=[
            pltpu.VMEM((nch, DH_M, DH_M + LANES), F32),
            pltpu.VMEM((nch, 1, 1), F32),
            pltpu.VMEM((nch, N, L), F32),
            pltpu.VMEM((nch, N, L), F32),
        ],
        compiler_params=pltpu.CompilerParams(
            dimension_semantics=("arbitrary", "arbitrary"),
            vmem_limit_bytes=_vmem_limit(blocks, scratch)),
        name="mlstm_bwd" if reverse else "mlstm_fwd",
    )(*args)


def _outproj_kernel(x_ref, ya_ref, ym_ref, wa_ref, wm_ref, o_ref):
    o_ref[...] = (x_ref[...]
                  + jnp.dot(ya_ref[...], wa_ref[...], preferred_element_type=F32)
                  + jnp.dot(ym_ref[...], wm_ref[...], preferred_element_type=F32))


def _outproj(x, ya, ym, w_a, w_m):
    T, D = x.shape
    blocks = (_nbytes((TM, D), F32) * 2 + _nbytes((TM, W_A), BF16) * 2 + _nbytes((W_A, D), BF16) * 2)
    return pl.pallas_call(
        _outproj_kernel,
        grid=(T // TM,),
        in_specs=[
            pl.BlockSpec((TM, D), lambda i: (i, 0)),
            pl.BlockSpec((TM, W_A), lambda i: (i, 0)),
            pl.BlockSpec((TM, W_M), lambda i: (i, 0)),
            pl.BlockSpec((W_A, D), lambda i: (0, 0)),
            pl.BlockSpec((W_M, D), lambda i: (0, 0)),
        ],
        out_specs=pl.BlockSpec((TM, D), lambda i: (i, 0)),
        out_shape=jax.ShapeDtypeStruct((T, D), F32),
        compiler_params=pltpu.CompilerParams(
            dimension_semantics=("arbitrary",),
            vmem_limit_bytes=_vmem_limit(blocks, 0)),
        name="outproj",
    )(x, ya, ym, w_a, w_m)


def _ple_kernel(x_ref, g_ref, pe_ref, wg_ref, wp_ref, o_ref):
    x = x_ref[...]
    hn = _rms_rows(x, g_ref[...]).astype(BF16)
    gate = jax.nn.sigmoid(jnp.dot(hn, wg_ref[...], preferred_element_type=F32))
    proj = jnp.dot(pe_ref[...].astype(BF16), wp_ref[...], preferred_element_type=F32)
    o_ref[...] = x + gate * proj


def _ple(x, g, pe, w_gate, w_proj):
    T, D = x.shape
    P = pe.shape[1]
    blocks = (_nbytes((TM, D), F32) * 2 + _nbytes((TM, P), F32) + _nbytes((D, D), BF16)
              + _nbytes((P, D), BF16))
    return pl.pallas_call(
        _ple_kernel,
        grid=(T // TM,),
        in_specs=[
            pl.BlockSpec((TM, D), lambda i: (i, 0)),
            pl.BlockSpec((1, D), lambda i: (0, 0)),
            pl.BlockSpec((TM, P), lambda i: (i, 0)),
            pl.BlockSpec((D, D), lambda i: (0, 0)),
            pl.BlockSpec((P, D), lambda i: (0, 0)),
        ],
        out_specs=pl.BlockSpec((TM, D), lambda i: (i, 0)),
        out_shape=jax.ShapeDtypeStruct((T, D), F32),
        compiler_params=pltpu.CompilerParams(
            dimension_semantics=("arbitrary",),
            vmem_limit_bytes=_vmem_limit(blocks, 0)),
        name="ple",
    )(x, g.reshape(1, D), pe, w_gate, w_proj)


def _layer(x, pe, w):
    B, S, D = x.shape
    T = B * S
    x2 = x.reshape(T, D)
    x2 = _ffn(x2, w["g_ffn1"], w["w1g"], w["w1u"], w["w1d"])

    u, gates, kt = _inproj(x2, w["g_mix"], w["w_in_main"], w["w_in_gate"], w["w_in_kt"],
                           w["g_qn"], w["g_kn"])
    u = u.reshape(B, S, U_MAIN)
    ya = _attention(u, w["attn_bias"], B, S)

    n_chunks = S // ML_CHUNK
    gates_t = jnp.transpose(gates[:, :N_GATES].reshape(B, S, N_GATES), (0, 2, 1))
    gates_t = gates_t.reshape(B, N_GATES, n_chunks, ML_CHUNK)
    hf = _mlstm_sweep(u, kt, gates_t, w["b_igate"], w["b_fgate"], B, S, reverse=False)
    ym = _mlstm_sweep(u, kt, gates_t, w["b_igate"], w["b_fgate"], B, S, reverse=True,
                      hf=hf, g_mh=w["g_mh"])

    x2 = _outproj(x2, ya.reshape(T, W_A), ym.reshape(T, W_M), w["w_out_a"], w["w_out_m"])
    x2 = _ffn(x2, w["g_ffn2"], w["w2g"], w["w2u"], w["w2d"])
    x2 = _ple(x2, w["g_ple"], pe.reshape(T, -1), w["w_ple_gate"], w["w_ple_proj"])
    return x2.reshape(B, S, D)


def kernel(x_prompt, x_sample, p_prompt, p_sample, g_ffn1, w_ffn1_gate, w_ffn1_up, w_ffn1_down, g_mix, w_in, b_igate, b_fgate, g_qn, g_kn, rpb, g_mh, w_out, g_ffn2, w_ffn2_gate, w_ffn2_up, w_ffn2_down, g_ple, w_ple_gate, w_ple_proj):
    depth = g_ffn1.shape[0]
    xs = [x_prompt, x_sample]
    ps = [p_prompt, p_sample]
    for i in range(depth):
        w_in_i = w_in[i]
        k_lo = 3 * W_A + W_M
        k_hi = k_lo + W_M
        u_end = 3 * W_A + 4 * W_M
        gate_cols = jnp.pad(w_in_i[:, u_end:], ((0, 0), (0, LANES - N_GATES)))
        w = {
            "g_ffn1": g_ffn1[i],
            "w1g": w_ffn1_gate[i].astype(BF16), "w1u": w_ffn1_up[i].astype(BF16),
            "w1d": w_ffn1_down[i],
            "g_mix": g_mix[i],
            "w_in_main": w_in_i.astype(BF16),
            "w_in_gate": gate_cols.astype(BF16),
            "w_in_kt": (w_in_i[:, k_lo:k_hi] * (DH_M ** -0.5)).T.astype(BF16),
            "b_igate": b_igate[i], "b_fgate": b_fgate[i], "g_qn": g_qn[i], "g_kn": g_kn[i],
            "attn_bias": _attn_bias_table(rpb[i]), "g_mh": g_mh[i],
            "w_out_a": w_out[i, :W_A].astype(BF16), "w_out_m": w_out[i, W_A:].astype(BF16),
            "g_ffn2": g_ffn2[i],
            "w2g": w_ffn2_gate[i].astype(BF16), "w2u": w_ffn2_up[i].astype(BF16),
            "w2d": w_ffn2_down[i],
            "g_ple": g_ple[i], "w_ple_gate": w_ple_gate[i].astype(BF16),
            "w_ple_proj": w_ple_proj[i].astype(BF16),
        }
        xs = [_layer(x, p[i], w) for x, p in zip(xs, ps)]
    return (xs[0], xs[1])
```

```python
import functools

import numpy as np
import jax
import jax.numpy as jnp
from jax import lax
from jax.experimental import pallas as pl
from jax.experimental.pallas import tpu as pltpu

F32 = jnp.float32
BF16 = jnp.bfloat16

GRID_W = 64
WIN_H = 8
WIN_W = 16
H_A = 8
DH_A = 128
H_M = 4
DH_M = 256
W_A = H_A * DH_A
W_M = H_M * DH_M
EPS = 1e-6
N_GATES = 4 * H_M
U_MAIN = 3 * W_A + 3 * W_M

LANES = 128
V7X_VMEM_BYTES = 64 * 1024 * 1024

TM = 512
TM_FFN = 1024
TM_IN = 1024
TF = 512
TN_IN = 1024
ATT_RB = 4
ATT_WIN = 12
ATT_HEADS = 8
ML_CHUNK = 256
ML_BATCH = 1
NEG_BIG = -1e30


def _vmem_limit(block_bytes, scratch_bytes):
    est = 2 * block_bytes + scratch_bytes + 12 * 1024 * 1024
    return int(min(est, V7X_VMEM_BYTES - 6 * 1024 * 1024))


def _nbytes(shape, dtype):
    return int(np.prod(shape)) * jnp.dtype(dtype).itemsize


def _rms_rows(x, g):
    ms = jnp.mean(x * x, axis=-1, keepdims=True)
    return x * lax.rsqrt(ms + EPS) * g


def _ffn_kernel(x_ref, g_ref, wg_ref, wu_ref, wd_ref, o_ref, hn_ref):
    j = pl.program_id(1)

    def half_step(hn):
        a = jnp.dot(hn, wg_ref[...], preferred_element_type=F32)
        b = jnp.dot(hn, wu_ref[...], preferred_element_type=F32)
        h = (a * jax.nn.sigmoid(a) * (0.5 * b)).astype(BF16)
        return jnp.dot(h, wd_ref[...].astype(BF16), preferred_element_type=F32)

    @pl.when(j == 0)
    def _():
        x = x_ref[...]
        hn = _rms_rows(x, g_ref[...]).astype(BF16)
        hn_ref[...] = hn
        o_ref[...] = x + half_step(hn)

    @pl.when(j > 0)
    def _():
        o_ref[...] += half_step(hn_ref[...])


def _ffn(x, g, wg, wu, wd):
    T, D = x.shape
    F = wd.shape[0]
    tm = TM_FFN
    blocks = (_nbytes((tm, D), F32) * 2 + _nbytes((D, 2 * TF), BF16) + _nbytes((TF, D), F32))
    scratch = _nbytes((tm, D), BF16)
    return pl.pallas_call(
        _ffn_kernel,
        grid=(T // tm, F // TF),
        in_specs=[
            pl.BlockSpec((tm, D), lambda i, j: (i, 0)),
            pl.BlockSpec((1, D), lambda i, j: (0, 0)),
            pl.BlockSpec((D, TF), lambda i, j: (0, j)),
            pl.BlockSpec((D, TF), lambda i, j: (0, j)),
            pl.BlockSpec((TF, D), lambda i, j: (j, 0)),
        ],
        out_specs=pl.BlockSpec((tm, D), lambda i, j: (i, 0)),
        out_shape=jax.ShapeDtypeStruct((T, D), F32),
        scratch_shapes=[pltpu.VMEM((tm, D), BF16)],
        compiler_params=pltpu.CompilerParams(
            dimension_semantics=("arbitrary", "arbitrary"),
            vmem_limit_bytes=_vmem_limit(blocks, scratch)),
        name="ffn",
    )(x, g.reshape(1, D), wg, wu, wd)


_NT = (((1,), (1,)), ((), ()))


def _inproj_kernel(x_ref, g_ref, wt_ref, wgate_ref, gq_ref, gk_ref,
                   u_ref, gates_ref, kt_ref, hn_ref, *, n_u_tiles):
    j = pl.program_id(1)

    n_q_tiles = W_A // TN_IN
    n_qk_tiles = 2 * n_q_tiles

    def project(hn):
        return lax.dot_general(hn, wt_ref[...], _NT, preferred_element_type=F32)

    def store_qk_normed(acc, gain):
        for hh in range(TN_IN // DH_A):
            sl = slice(hh * DH_A, (hh + 1) * DH_A)
            u_ref[:, sl] = _rms_rows(acc[:, sl], gain).astype(BF16)

    @pl.when(j == 0)
    def _():
        hn = _rms_rows(x_ref[...], g_ref[...]).astype(BF16)
        hn_ref[...] = hn
        gates_ref[...] = lax.dot_general(hn, wgate_ref[...], _NT, preferred_element_type=F32)
        store_qk_normed(project(hn), gq_ref[...] * (DH_A ** -0.5))

    @pl.when((j > 0) & (j < n_qk_tiles))
    def _():
        gain = jnp.where(j < n_q_tiles, gq_ref[...] * (DH_A ** -0.5), gk_ref[...])
        store_qk_normed(project(hn_ref[...]), gain)

    @pl.when((j >= n_qk_tiles) & (j < n_u_tiles))
    def _():
        u_ref[...] = project(hn_ref[...]).astype(BF16)

    @pl.when(j == n_u_tiles)
    def _():
        kt = lax.dot_general(wt_ref[...], hn_ref[...], _NT, preferred_element_type=F32)
        kt_ref[...] = (kt * (DH_M ** -0.5)).astype(BF16)


def _inproj(x, g, w_t, w_gate_t, g_qn, g_kn):
    T, D = x.shape
    tm = TM_IN
    nu = U_MAIN // TN_IN
    assert W_M == TN_IN
    blocks = (_nbytes((tm, D), F32) + _nbytes((TN_IN, D), BF16) + _nbytes((LANES, D), BF16)
              + _nbytes((tm, TN_IN), BF16) + _nbytes((tm, LANES), F32) + _nbytes((W_M, tm), BF16))
    scratch = _nbytes((tm, D), BF16)
    last = nu - 1
    km_block = (3 * W_A + W_M) // TN_IN

    def w_row_block(j):
        return jnp.where(j < km_block, j, jnp.where(j < nu, j + 1, km_block))

    return pl.pallas_call(
        functools.partial(_inproj_kernel, n_u_tiles=nu),
        grid=(T // tm, nu + 1),
        in_specs=[
            pl.BlockSpec((tm, D), lambda i, j: (i, 0)),
            pl.BlockSpec((1, D), lambda i, j: (0, 0)),
            pl.BlockSpec((TN_IN, D), lambda i, j: (w_row_block(j), 0)),
            pl.BlockSpec((LANES, D), lambda i, j: (0, 0)),
            pl.BlockSpec((1, DH_A), lambda i, j: (0, 0)),
            pl.BlockSpec((1, DH_A), lambda i, j: (0, 0)),
        ],
        out_specs=[
            pl.BlockSpec((tm, TN_IN), lambda i, j: (i, jnp.minimum(j, last))),
            pl.BlockSpec((tm, LANES), lambda i, j: (i, 0)),
            pl.BlockSpec((W_M, tm), lambda i, j: (0, i)),
        ],
        out_shape=[
            jax.ShapeDtypeStruct((T, U_MAIN), BF16),
            jax.ShapeDtypeStruct((T, LANES), F32),
            jax.ShapeDtypeStruct((W_M, T), BF16),
        ],
        scratch_shapes=[pltpu.VMEM((tm, D), BF16)],
        compiler_params=pltpu.CompilerParams(
            dimension_semantics=("arbitrary", "arbitrary"),
            vmem_limit_bytes=_vmem_limit(blocks, scratch)),
        name="inproj",
    )(x, g.reshape(1, D), w_t, w_gate_t, g_qn.reshape(1, DH_A), g_kn.reshape(1, DH_A))


def _attn_bias_table(rpb):
    a = np.arange(ATT_RB)[:, None]
    ik = np.arange(ATT_WIN)[None, :]
    kh = WIN_H
    valid0 = ik < kh
    dr0 = ik - a + (WIN_H - 1)
    valid1 = (ik >= a) & (ik < a + kh)
    dr1 = ik - a + (WIN_H - 1) - kh // 2
    valid2 = (ik >= ATT_WIN - kh) & (ik < ATT_WIN)
    dr2 = (ik - ATT_WIN) - (a - ATT_RB) + (WIN_H - 1)
    valid = np.stack([np.broadcast_to(valid0, dr0.shape), valid1,
                      np.broadcast_to(valid2, dr0.shape)])
    dr = np.stack([dr0, dr1, dr2])

    c = np.arange(GRID_W)
    cs = np.clip(c - WIN_W // 2, 0, GRID_W - WIN_W)
    colmask = (c[None, :] >= cs[:, None]) & (c[None, :] < cs[:, None] + WIN_W)
    dc = np.clip(c[None, :] - c[:, None], -(WIN_W - 1), WIN_W - 1) + (WIN_W - 1)

    onehot = (dc[None] == np.arange(2 * WIN_W - 1)[:, None, None]).astype(np.float32)
    tiles = jnp.einsum("hrc,cqk->hqrk", rpb.astype(F32), jnp.asarray(onehot),
                       precision=lax.Precision.HIGHEST)
    tiles = jnp.where(jnp.asarray(colmask)[:, None, :], tiles, NEG_BIG)
    seq = tiles.reshape(H_A, GRID_W, (2 * WIN_H - 1) * GRID_W)
    classes = []
    for cls in range(3):
        rows = []
        for qa in range(ATT_RB):
            kk = np.nonzero(valid[cls, qa])[0]
            k0, k1 = int(kk[0]), int(kk[-1]) + 1
            d0 = int(dr[cls, qa, k0])
            assert np.array_equal(kk, np.arange(k0, k1))
            assert np.array_equal(dr[cls, qa, k0:k1], np.arange(d0, d0 + k1 - k0))
            strip = seq[:, :, d0 * GRID_W:(d0 + k1 - k0) * GRID_W]
            rows.append(jnp.pad(strip, ((0, 0), (0, 0), (k0 * GRID_W, (ATT_WIN - k1) * GRID_W)),
                                constant_values=NEG_BIG))
        classes.append(jnp.concatenate(rows, axis=1))
    return jnp.stack(classes)


def _attn_kernel(q_ref, k_ref, v_ref, bias_ref, o_ref, *, n_rows):
    i = pl.program_id(2)
    nb = n_rows // ATT_RB
    cls = jnp.where(i == 0, 0, jnp.where(i == nb - 1, 2, 1))

    for hh in range(ATT_HEADS):
        sl = slice(hh * DH_A, (hh + 1) * DH_A)
        q = q_ref[0, :, sl]
        kw = k_ref[0, :, sl]
        vw = v_ref[0, :, sl]
        s = lax.dot_general(q, kw, (((1,), (1,)), ((), ())), preferred_element_type=F32)
        s = s + bias_ref[cls, hh]
        m = jnp.max(s, axis=-1, keepdims=True)
        p = jnp.exp(s - m)
        l = jnp.sum(p, axis=-1, keepdims=True)
        o = jnp.dot(p.astype(BF16), vw, preferred_element_type=F32)
        o_ref[0, :, sl] = (o / l).astype(BF16)


def _attention(u, bias, B, S):
    R = S // GRID_W
    assert R % ATT_RB == 0 and R >= ATT_WIN and R // ATT_RB >= 3
    tq = ATT_RB * GRID_W
    tk = ATT_WIN * GRID_W
    hw = ATT_HEADS * DH_A
    ng = H_A // ATT_HEADS
    blocks = (_nbytes((tq, hw), BF16) * 2 + _nbytes((tk, hw), BF16) * 2
              + _nbytes((3, ATT_HEADS, tq, tk), F32))

    def win_start(i):
        return jnp.clip(i * ATT_RB - WIN_H // 2, 0, R - ATT_WIN) * GRID_W

    return pl.pallas_call(
        functools.partial(_attn_kernel, n_rows=R),
        grid=(ng, B, R // ATT_RB),
        in_specs=[
            pl.BlockSpec((1, tq, hw), lambda h, b, i: (b, i, h)),
            pl.BlockSpec((pl.Element(1), pl.Element(tk), pl.Element(hw)),
                         lambda h, b, i: (b, win_start(i), (ng + h) * hw)),
            pl.BlockSpec((pl.Element(1), pl.Element(tk), pl.Element(hw)),
                         lambda h, b, i: (b, win_start(i), (2 * ng + h) * hw)),
            pl.BlockSpec((3, ATT_HEADS, tq, tk), lambda h, b, i: (0, h, 0, 0)),
        ],
        out_specs=pl.BlockSpec((1, tq, hw), lambda h, b, i: (b, i, h)),
        out_shape=jax.ShapeDtypeStruct((B, S, W_A), BF16),
        compiler_params=pltpu.CompilerParams(
            dimension_semantics=("arbitrary", "arbitrary", "arbitrary"),
            vmem_limit_bytes=_vmem_limit(blocks, 0)),
        name="attention",
    )(u, u, u, bias)


def _mlstm_kernel(*refs, reverse, n_chunks):
    bi_ref, bf_ref, q_ref = refs[:3]
    kt_refs = refs[3:3 + ML_BATCH]
    rest = refs[3 + ML_BATCH:]
    if reverse:
        (v_ref, gi_ref, gf_ref, hf_ref, og_ref, gmh_ref,
         o_ref, ct_ref, m_ref, b_ref, e_ref) = rest
    else:
        (v_ref, gi_ref, gf_ref, o_ref, ct_ref, m_ref, b_ref, e_ref) = rest
    L = ML_CHUNK
    c = pl.program_id(1)
    d = 1 if reverse else 0

    row = lax.broadcasted_iota(jnp.int32, (L, L), 0)
    col = lax.broadcasted_iota(jnp.int32, (L, L), 1)
    tri = (col >= row) if reverse else (col <= row)
    eye = row == col

    @pl.when(c == 0)
    def _():
        ct_ref[...] = jnp.zeros_like(ct_ref)
        m_ref[...] = jnp.zeros_like(m_ref)
        cum = jnp.where((row >= col) if reverse else (row <= col), 1.0, 0.0).astype(F32)
        for bb in range(ML_BATCH):
            for hh in range(H_M):
                ig = gi_ref[bb, hh] + bi_ref[d, hh]
                lf = jax.nn.log_sigmoid(gf_ref[bb, hh] + bf_ref[d, hh])
                b = jnp.dot(lf, cum, preferred_element_type=F32, precision=lax.Precision.HIGHEST)
                b_ref[bb * H_M + hh] = b
                e_ref[bb * H_M + hh] = ig - b

    cc = (n_chunks - 1 - c) if reverse else c

    for ch in range(ML_BATCH * H_M):
        bb, hh = divmod(ch, H_M)
        sl = slice(hh * DH_M, (hh + 1) * DH_M)
        b_row = b_ref[ch, pl.ds(cc, 1), :]
        e_row = e_ref[ch, pl.ds(cc, 1), :]
        b_col = jnp.sum(jnp.where(eye, b_row, 0.0), axis=1, keepdims=True)
        e_col = jnp.sum(jnp.where(eye, e_row, 0.0), axis=1, keepdims=True)
        a_tot = b_row[:, 0:1] if reverse else b_row[:, L - 1:L]
        m_prev = m_ref[ch]

        q = q_ref[bb, :, sl]
        kt = kt_refs[bb][sl, :]
        v = v_ref[bb, :, sl]

        dmat = jnp.where(tri, b_col + e_row, -jnp.inf)
        inter = b_col + m_prev
        m_t = jnp.maximum(inter, jnp.max(dmat, axis=1, keepdims=True))
        dexp = jnp.exp(dmat - m_t)
        s = jnp.dot(q, kt, preferred_element_type=F32)
        sqk = s * dexp
        sc = jnp.exp(inter - m_t)
        ct = ct_ref[ch]
        inter_aug = jnp.dot(q, ct.astype(BF16), preferred_element_type=F32)
        num = sc * inter_aug[:, :DH_M] + jnp.dot(sqk.astype(BF16), v, preferred_element_type=F32)
        den = sc * inter_aug[:, DH_M:] + jnp.sum(sqk, axis=1, keepdims=True)
        rden = 1.0 / jnp.maximum(jnp.abs(den), jnp.exp(-m_t))
        hdir = num * jnp.concatenate([rden] * (DH_M // LANES), axis=1)

        g_col = a_tot + e_col
        m_loc = jnp.max(g_col, axis=0, keepdims=True)
        m_new = jnp.maximum(a_tot + m_prev, m_loc)
        s_prev = jnp.exp(a_tot + m_prev - m_new)
        w_col = jnp.exp(g_col - m_new)
        vw_aug = jnp.concatenate(
            [(v.astype(F32) * w_col).astype(BF16),
             jnp.broadcast_to(w_col, (L, LANES)).astype(BF16)], axis=1)
        ct_ref[ch] = s_prev * ct + jnp.dot(kt, vw_aug, preferred_element_type=F32)
        m_ref[ch] = m_new

        if reverse:
            hm = _rms_rows(hf_ref[bb, :, sl] + hdir, gmh_ref[:, sl])
            o_ref[bb, :, sl] = (jax.nn.sigmoid(og_ref[bb, :, sl].astype(F32)) * hm).astype(BF16)
        else:
            o_ref[bb, :, sl] = hdir


def _mlstm_sweep(u, kt, gates_t, b_igate, b_fgate, B, S, *, reverse, hf=None, g_mh=None):
    L = ML_CHUNK
    N = S // L
    d = 1 if reverse else 0
    qoff = 3 * W_A // W_M
    cidx = (lambda c: N - 1 - c) if reverse else (lambda c: c)
    smem = pl.BlockSpec(memory_space=pltpu.SMEM)
    nb = ML_BATCH
    assert B % nb == 0
    nch = nb * H_M
    kt_specs = [pl.BlockSpec((W_M, L), functools.partial(
        lambda b, c, r: (0, (nb * b + r) * N + cidx(c)), r=r)) for r in range(nb)]
    in_specs = [
        smem, smem,
        pl.BlockSpec((nb, L, W_M), lambda b, c: (b, cidx(c), qoff)),
        *kt_specs,
        pl.BlockSpec((nb, L, W_M), lambda b, c: (b, cidx(c), qoff + 1)),
        pl.BlockSpec((nb, H_M, N, L), lambda b, c: (b, d, 0, 0)),
        pl.BlockSpec((nb, H_M, N, L), lambda b, c: (b, 2 + d, 0, 0)),
    ]
    args = [b_igate, b_fgate, u, *([kt] * nb), u, gates_t, gates_t]
    if reverse:
        in_specs += [
            pl.BlockSpec((nb, L, W_M), lambda b, c: (b, cidx(c), 0)),
            pl.BlockSpec((nb, L, W_M), lambda b, c: (b, cidx(c), qoff + 2)),
            pl.BlockSpec((1, W_M), lambda b, c: (0, 0)),
        ]
        args += [hf, u, g_mh.reshape(1, W_M)]
        out_dtype = BF16
    else:
        out_dtype = F32
    blocks = nb * (_nbytes((L, W_M), BF16) * 4 + _nbytes((H_M, N, L), F32) * 2
                   + _nbytes((L, W_M), F32) * 2)
    scratch = (nch * _nbytes((DH_M, DH_M + LANES), F32) + 2 * nch * _nbytes((N, L), F32)
               + nch * 8 * LANES * 4)
    return pl.pallas_call(
        functools.partial(_mlstm_kernel, reverse=reverse, n_chunks=N),
        grid=(B // nb, N),
        in_specs=in_specs,
        out_specs=pl.BlockSpec((nb, L, W_M), lambda b, c: (b, cidx(c), 0)),
        out_shape=jax.ShapeDtypeStruct((B, S, W_M), out_dtype),
        scratch_shapes=[
            pltpu.VMEM((nch, DH_M, DH_M + LANES), F32),
            pltpu.VMEM((nch, 1, 1), F32),
            pltpu.VMEM((nch, N, L), F32),
            pltpu.VMEM((nch, N, L), F32),
        ],
        compiler_params=pltpu.CompilerParams(
            dimension_semantics=("arbitrary", "arbitrary"),
            vmem_limit_bytes=_vmem_limit(blocks, scratch)),
        name="mlstm_bwd" if reverse else "mlstm_fwd",
    )(*args)


def _outproj_kernel(x_ref, ya_ref, ym_ref, wa_ref, wm_ref, o_ref):
    o_ref[...] = (x_ref[...]
                  + jnp.dot(ya_ref[...], wa_ref[...], preferred_element_type=F32)
                  + jnp.dot(ym_ref[...], wm_ref[...], preferred_element_type=F32))


def _outproj(x, ya, ym, w_a, w_m):
    T, D = x.shape
    blocks = (_nbytes((TM, D), F32) * 2 + _nbytes((TM, W_A), BF16) * 2 + _nbytes((W_A, D), BF16) * 2)
    return pl.pallas_call(
        _outproj_kernel,
        grid=(T // TM,),
        in_specs=[
            pl.BlockSpec((TM, D), lambda i: (i, 0)),
            pl.BlockSpec((TM, W_A), lambda i: (i, 0)),
            pl.BlockSpec((TM, W_M), lambda i: (i, 0)),
            pl.BlockSpec((W_A, D), lambda i: (0, 0)),
            pl.BlockSpec((W_M, D), lambda i: (0, 0)),
        ],
        out_specs=pl.BlockSpec((TM, D), lambda i: (i, 0)),
        out_shape=jax.ShapeDtypeStruct((T, D), F32),
        compiler_params=pltpu.CompilerParams(
            dimension_semantics=("arbitrary",),
            vmem_limit_bytes=_vmem_limit(blocks, 0)),
        name="outproj",
    )(x, ya, ym, w_a, w_m)


def _ple_kernel(x_ref, g_ref, pe_ref, wg_ref, wp_ref, o_ref):
    x = x_ref[...]
    hn = _rms_rows(x, g_ref[...]).astype(BF16)
    gate = jax.nn.sigmoid(jnp.dot(hn, wg_ref[...], preferred_element_type=F32))
    proj = jnp.dot(pe_ref[...].astype(BF16), wp_ref[...], preferred_element_type=F32)
    o_ref[...] = x + gate * proj


def _ple(x, g, pe, w_gate, w_proj):
    T, D = x.shape
    P = pe.shape[1]
    blocks = (_nbytes((TM, D), F32) * 2 + _nbytes((TM, P), F32) + _nbytes((D, D), BF16)
              + _nbytes((P, D), BF16))
    return pl.pallas_call(
        _ple_kernel,
        grid=(T // TM,),
        in_specs=[
            pl.BlockSpec((TM, D), lambda i: (i, 0)),
            pl.BlockSpec((1, D), lambda i: (0, 0)),
            pl.BlockSpec((TM, P), lambda i: (i, 0)),
            pl.BlockSpec((D, D), lambda i: (0, 0)),
            pl.BlockSpec((P, D), lambda i: (0, 0)),
        ],
        out_specs=pl.BlockSpec((TM, D), lambda i: (i, 0)),
        out_shape=jax.ShapeDtypeStruct((T, D), F32),
        compiler_params=pltpu.CompilerParams(
            dimension_semantics=("arbitrary",),
            vmem_limit_bytes=_vmem_limit(blocks, 0)),
        name="ple",
    )(x, g.reshape(1, D), pe, w_gate, w_proj)


def _layer(x, pe, w):
    B, S, D = x.shape
    T = B * S
    x2 = x.reshape(T, D)
    x2 = _ffn(x2, w["g_ffn1"], w["w1g"], w["w1u"], w["w1d"])

    u, gates, kt = _inproj(x2, w["g_mix"], w["w_in_t"], w["w_in_gate_t"], w["g_qn"], w["g_kn"])
    u = u.reshape(B, S, U_MAIN)
    ya = _attention(u, w["attn_bias"], B, S)

    n_chunks = S // ML_CHUNK
    gates_t = jnp.transpose(gates[:, :N_GATES].reshape(B, S, N_GATES), (0, 2, 1))
    gates_t = gates_t.reshape(B, N_GATES, n_chunks, ML_CHUNK)
    hf = _mlstm_sweep(u, kt, gates_t, w["b_igate"], w["b_fgate"], B, S, reverse=False)
    ym = _mlstm_sweep(u, kt, gates_t, w["b_igate"], w["b_fgate"], B, S, reverse=True,
                      hf=hf, g_mh=w["g_mh"])

    x2 = _outproj(x2, ya.reshape(T, W_A), ym.reshape(T, W_M), w["w_out_a"], w["w_out_m"])
    x2 = _ffn(x2, w["g_ffn2"], w["w2g"], w["w2u"], w["w2d"])
    x2 = _ple(x2, w["g_ple"], pe.reshape(T, -1), w["w_ple_gate"], w["w_ple_proj"])
    return x2.reshape(B, S, D)


def kernel(x_prompt, x_sample, p_prompt, p_sample, g_ffn1, w_ffn1_gate, w_ffn1_up, w_ffn1_down, g_mix, w_in, b_igate, b_fgate, g_qn, g_kn, rpb, g_mh, w_out, g_ffn2, w_ffn2_gate, w_ffn2_up, w_ffn2_down, g_ple, w_ple_gate, w_ple_proj):
    depth = g_ffn1.shape[0]
    xs = [x_prompt, x_sample]
    ps = [p_prompt, p_sample]
    for i in range(depth):
        w_in_t = jnp.transpose(w_in[i]).astype(BF16)
        u_end = 3 * W_A + 4 * W_M
        gate_rows = jnp.pad(w_in_t[u_end:], ((0, LANES - N_GATES), (0, 0)))
        w = {
            "g_ffn1": g_ffn1[i],
            "w1g": w_ffn1_gate[i].astype(BF16), "w1u": w_ffn1_up[i].astype(BF16),
            "w1d": w_ffn1_down[i],
            "g_mix": g_mix[i],
            "w_in_t": w_in_t,
            "w_in_gate_t": gate_rows,
            "b_igate": b_igate[i], "b_fgate": b_fgate[i], "g_qn": g_qn[i], "g_kn": g_kn[i],
            "attn_bias": _attn_bias_table(rpb[i]), "g_mh": g_mh[i],
            "w_out_a": w_out[i, :W_A].astype(BF16), "w_out_m": w_out[i, W_A:].astype(BF16),
            "g_ffn2": g_ffn2[i],
            "w2g": w_ffn2_gate[i].astype(BF16), "w2u": w_ffn2_up[i].astype(BF16),
            "w2d": w_ffn2_down[i],
            "g_ple": g_ple[i], "w_ple_gate": w_ple_gate[i].astype(BF16),
            "w_ple_proj": w_ple_proj[i].astype(BF16),
        }
        xs = [_layer(x, p[i], w) for x, p in zip(xs, ps)]
    return (xs[0], xs[1])
```

```python
import functools

import numpy as np
import jax
import jax.numpy as jnp
from jax import lax
from jax.experimental import pallas as pl
from jax.experimental.pallas import tpu as pltpu

F32 = jnp.float32
BF16 = jnp.bfloat16

GRID_W = 64
WIN_H = 8
WIN_W = 16
H_A = 8
DH_A = 128
H_M = 4
DH_M = 256
W_A = H_A * DH_A
W_M = H_M * DH_M
EPS = 1e-6
N_GATES = 4 * H_M
U_MAIN = 3 * W_A + 3 * W_M

LANES = 128
V7X_VMEM_BYTES = 64 * 1024 * 1024

TM = 512
TM_FFN = 1024
TM_IN = 1024
TF = 512
TN_IN = 1024
ATT_RB = 4
ATT_WIN = 12
ATT_HEADS = 8
ML_CHUNK = 256
ML_BATCH = 1
NEG_BIG = -1e30


def _vmem_limit(block_bytes, scratch_bytes):
    est = 2 * block_bytes + scratch_bytes + 12 * 1024 * 1024
    return int(min(est, V7X_VMEM_BYTES - 6 * 1024 * 1024))


def _nbytes(shape, dtype):
    return int(np.prod(shape)) * jnp.dtype(dtype).itemsize


def _rms_rows(x, g):
    ms = jnp.mean(x * x, axis=-1, keepdims=True)
    return x * lax.rsqrt(ms + EPS) * g


def _ffn_kernel(x_ref, g_ref, wg_ref, wu_ref, wd_ref, o_ref, hn_ref):
    j = pl.program_id(1)

    def half_step(hn):
        a = jnp.dot(hn, wg_ref[...], preferred_element_type=F32)
        b = jnp.dot(hn, wu_ref[...], preferred_element_type=F32)
        h = (a * jax.nn.sigmoid(a) * (0.5 * b)).astype(BF16)
        return jnp.dot(h, wd_ref[...].astype(BF16), preferred_element_type=F32)

    @pl.when(j == 0)
    def _():
        x = x_ref[...]
        hn = _rms_rows(x, g_ref[...]).astype(BF16)
        hn_ref[...] = hn
        o_ref[...] = x + half_step(hn)

    @pl.when(j > 0)
    def _():
        o_ref[...] += half_step(hn_ref[...])


def _ffn(x, g, wg, wu, wd):
    T, D = x.shape
    F = wd.shape[0]
    tm = TM_FFN
    blocks = (_nbytes((tm, D), F32) * 2 + _nbytes((D, 2 * TF), BF16) + _nbytes((TF, D), F32))
    scratch = _nbytes((tm, D), BF16)
    return pl.pallas_call(
        _ffn_kernel,
        grid=(T // tm, F // TF),
        in_specs=[
            pl.BlockSpec((tm, D), lambda i, j: (i, 0)),
            pl.BlockSpec((1, D), lambda i, j: (0, 0)),
            pl.BlockSpec((D, TF), lambda i, j: (0, j)),
            pl.BlockSpec((D, TF), lambda i, j: (0, j)),
            pl.BlockSpec((TF, D), lambda i, j: (j, 0)),
        ],
        out_specs=pl.BlockSpec((tm, D), lambda i, j: (i, 0)),
        out_shape=jax.ShapeDtypeStruct((T, D), F32),
        scratch_shapes=[pltpu.VMEM((tm, D), BF16)],
        compiler_params=pltpu.CompilerParams(
            dimension_semantics=("arbitrary", "arbitrary"),
            vmem_limit_bytes=_vmem_limit(blocks, scratch)),
        name="ffn",
    )(x, g.reshape(1, D), wg, wu, wd)


_NT = (((1,), (1,)), ((), ()))


def _inproj_kernel(x_ref, g_ref, wt_ref, wgate_ref, gq_ref, gk_ref,
                   u_ref, gates_ref, kt_ref, hn_ref, *, n_u_tiles):
    j = pl.program_id(1)

    n_q_tiles = W_A // TN_IN
    n_qk_tiles = 2 * n_q_tiles

    def project(hn):
        return lax.dot_general(hn, wt_ref[...], _NT, preferred_element_type=F32)

    def store_qk_normed(acc, gain):
        for hh in range(TN_IN // DH_A):
            sl = slice(hh * DH_A, (hh + 1) * DH_A)
            u_ref[:, sl] = _rms_rows(acc[:, sl], gain).astype(BF16)

    @pl.when(j == 0)
    def _():
        hn = _rms_rows(x_ref[...], g_ref[...]).astype(BF16)
        hn_ref[...] = hn
        gates_ref[...] = lax.dot_general(hn, wgate_ref[...], _NT, preferred_element_type=F32)
        store_qk_normed(project(hn), gq_ref[...] * (DH_A ** -0.5))

    @pl.when((j > 0) & (j < n_qk_tiles))
    def _():
        gain = jnp.where(j < n_q_tiles, gq_ref[...] * (DH_A ** -0.5), gk_ref[...])
        store_qk_normed(project(hn_ref[...]), gain)

    @pl.when((j >= n_qk_tiles) & (j < n_u_tiles))
    def _():
        u_ref[...] = project(hn_ref[...]).astype(BF16)

    @pl.when(j == n_u_tiles)
    def _():
        kt = lax.dot_general(wt_ref[...], hn_ref[...], _NT, preferred_element_type=F32)
        kt_ref[...] = (kt * (DH_M ** -0.5)).astype(BF16)


def _inproj(x, g, w_t, w_gate_t, g_qn, g_kn):
    T, D = x.shape
    tm = TM_IN
    nu = U_MAIN // TN_IN
    assert W_M == TN_IN
    blocks = (_nbytes((tm, D), F32) + _nbytes((TN_IN, D), BF16) + _nbytes((LANES, D), BF16)
              + _nbytes((tm, TN_IN), BF16) + _nbytes((tm, LANES), F32) + _nbytes((W_M, tm), BF16))
    scratch = _nbytes((tm, D), BF16)
    last = nu - 1
    km_block = (3 * W_A + W_M) // TN_IN

    def w_row_block(j):
        return jnp.where(j < km_block, j, jnp.where(j < nu, j + 1, km_block))

    return pl.pallas_call(
        functools.partial(_inproj_kernel, n_u_tiles=nu),
        grid=(T // tm, nu + 1),
        in_specs=[
            pl.BlockSpec((tm, D), lambda i, j: (i, 0)),
            pl.BlockSpec((1, D), lambda i, j: (0, 0)),
            pl.BlockSpec((TN_IN, D), lambda i, j: (w_row_block(j), 0)),
            pl.BlockSpec((LANES, D), lambda i, j: (0, 0)),
            pl.BlockSpec((1, DH_A), lambda i, j: (0, 0)),
            pl.BlockSpec((1, DH_A), lambda i, j: (0, 0)),
        ],
        out_specs=[
            pl.BlockSpec((tm, TN_IN), lambda i, j: (i, jnp.minimum(j, last))),
            pl.BlockSpec((tm, LANES), lambda i, j: (i, 0)),
            pl.BlockSpec((W_M, tm), lambda i, j: (0, i)),
        ],
        out_shape=[
            jax.ShapeDtypeStruct((T, U_MAIN), BF16),
            jax.ShapeDtypeStruct((T, LANES), F32),
            jax.ShapeDtypeStruct((W_M, T), BF16),
        ],
        scratch_shapes=[pltpu.VMEM((tm, D), BF16)],
        compiler_params=pltpu.CompilerParams(
            dimension_semantics=("arbitrary", "arbitrary"),
            vmem_limit_bytes=_vmem_limit(blocks, scratch)),
        name="inproj",
    )(x, g.reshape(1, D), w_t, w_gate_t, g_qn.reshape(1, DH_A), g_kn.reshape(1, DH_A))


def _attn_bias_table(rpb):
    a = np.arange(ATT_RB)[:, None]
    ik = np.arange(ATT_WIN)[None, :]
    kh = WIN_H
    valid0 = ik < kh
    dr0 = ik - a + (WIN_H - 1)
    valid1 = (ik >= a) & (ik < a + kh)
    dr1 = ik - a + (WIN_H - 1) - kh // 2
    valid2 = (ik >= ATT_WIN - kh) & (ik < ATT_WIN)
    dr2 = (ik - ATT_WIN) - (a - ATT_RB) + (WIN_H - 1)
    valid = np.stack([np.broadcast_to(valid0, dr0.shape), valid1,
                      np.broadcast_to(valid2, dr0.shape)])
    dr = np.stack([dr0, dr1, dr2])

    c = np.arange(GRID_W)
    cs = np.clip(c - WIN_W // 2, 0, GRID_W - WIN_W)
    colmask = (c[None, :] >= cs[:, None]) & (c[None, :] < cs[:, None] + WIN_W)
    dc = np.clip(c[None, :] - c[:, None], -(WIN_W - 1), WIN_W - 1) + (WIN_W - 1)

    onehot = (dc[None] == np.arange(2 * WIN_W - 1)[:, None, None]).astype(np.float32)
    tiles = jnp.einsum("hrc,cqk->hqrk", rpb.astype(F32), jnp.asarray(onehot),
                       precision=lax.Precision.HIGHEST)
    tiles = jnp.where(jnp.asarray(colmask)[:, None, :], tiles, NEG_BIG)
    seq = tiles.reshape(H_A, GRID_W, (2 * WIN_H - 1) * GRID_W)
    classes = []
    for cls in range(3):
        rows = []
        for qa in range(ATT_RB):
            kk = np.nonzero(valid[cls, qa])[0]
            k0, k1 = int(kk[0]), int(kk[-1]) + 1
            d0 = int(dr[cls, qa, k0])
            assert np.array_equal(kk, np.arange(k0, k1))
            assert np.array_equal(dr[cls, qa, k0:k1], np.arange(d0, d0 + k1 - k0))
            strip = seq[:, :, d0 * GRID_W:(d0 + k1 - k0) * GRID_W]
            rows.append(jnp.pad(strip, ((0, 0), (0, 0), (k0 * GRID_W, (ATT_WIN - k1) * GRID_W)),
                                constant_values=NEG_BIG))
        classes.append(jnp.concatenate(rows, axis=1))
    return jnp.stack(classes)


def _attn_kernel(q_ref, k_ref, v_ref, bias_ref, o_ref, *, n_rows):
    i = pl.program_id(2)
    nb = n_rows // ATT_RB
    cls = jnp.where(i == 0, 0, jnp.where(i == nb - 1, 2, 1))

    def scores(hh):
        sl = slice(hh * DH_A, (hh + 1) * DH_A)
        q = q_ref[0, :, sl]
        kw = k_ref[0, :, sl]
        s = lax.dot_general(q, kw, (((1,), (1,)), ((), ())), preferred_element_type=F32)
        s = s + bias_ref[cls, hh]
        return s, jnp.max(s, axis=-1, keepdims=True)

    def weighted_values(hh, s, m):
        sl = slice(hh * DH_A, (hh + 1) * DH_A)
        p = jnp.exp(s - m)
        l = jnp.sum(p, axis=-1, keepdims=True)
        o = jnp.dot(p.astype(BF16), v_ref[0, :, sl], preferred_element_type=F32)
        o_ref[0, :, sl] = (o / l).astype(BF16)

    pending = None
    for hh in range(ATT_HEADS):
        cur = scores(hh)
        if pending is not None:
            weighted_values(*pending)
        pending = (hh, *cur)
    weighted_values(*pending)


def _attention(u, bias, B, S):
    R = S // GRID_W
    assert R % ATT_RB == 0 and R >= ATT_WIN and R // ATT_RB >= 3
    tq = ATT_RB * GRID_W
    tk = ATT_WIN * GRID_W
    hw = ATT_HEADS * DH_A
    ng = H_A // ATT_HEADS
    blocks = (_nbytes((tq, hw), BF16) * 2 + _nbytes((tk, hw), BF16) * 2
              + _nbytes((3, ATT_HEADS, tq, tk), F32))

    def win_start(i):
        return jnp.clip(i * ATT_RB - WIN_H // 2, 0, R - ATT_WIN) * GRID_W

    return pl.pallas_call(
        functools.partial(_attn_kernel, n_rows=R),
        grid=(ng, B, R // ATT_RB),
        in_specs=[
            pl.BlockSpec((1, tq, hw), lambda h, b, i: (b, i, h)),
            pl.BlockSpec((pl.Element(1), pl.Element(tk), pl.Element(hw)),
                         lambda h, b, i: (b, win_start(i), (ng + h) * hw)),
            pl.BlockSpec((pl.Element(1), pl.Element(tk), pl.Element(hw)),
                         lambda h, b, i: (b, win_start(i), (2 * ng + h) * hw)),
            pl.BlockSpec((3, ATT_HEADS, tq, tk), lambda h, b, i: (0, h, 0, 0)),
        ],
        out_specs=pl.BlockSpec((1, tq, hw), lambda h, b, i: (b, i, h)),
        out_shape=jax.ShapeDtypeStruct((B, S, W_A), BF16),
        compiler_params=pltpu.CompilerParams(
            dimension_semantics=("arbitrary", "arbitrary", "arbitrary"),
            vmem_limit_bytes=_vmem_limit(blocks, 0)),
        name="attention",
    )(u, u, u, bias)


def _mlstm_kernel(*refs, reverse, n_chunks):
    bi_ref, bf_ref, q_ref = refs[:3]
    kt_refs = refs[3:3 + ML_BATCH]
    rest = refs[3 + ML_BATCH:]
    if reverse:
        (v_ref, gi_ref, gf_ref, hf_ref, og_ref, gmh_ref,
         o_ref, ct_ref, m_ref, b_ref, e_ref) = rest
    else:
        (v_ref, gi_ref, gf_ref, o_ref, ct_ref, m_ref, b_ref, e_ref) = rest
    L = ML_CHUNK
    c = pl.program_id(1)
    d = 1 if reverse else 0

    row = lax.broadcasted_iota(jnp.int32, (L, L), 0)
    col = lax.broadcasted_iota(jnp.int32, (L, L), 1)
    tri = (col >= row) if reverse else (col <= row)
    eye = row == col

    @pl.when(c == 0)
    def _():
        ct_ref[...] = jnp.zeros_like(ct_ref)
        m_ref[...] = jnp.zeros_like(m_ref)
        cum = jnp.where((row >= col) if reverse else (row <= col), 1.0, 0.0).astype(F32)
        for bb in range(ML_BATCH):
            for hh in range(H_M):
                ig = gi_ref[bb, hh] + bi_ref[d, hh]
                lf = jax.nn.log_sigmoid(gf_ref[bb, hh] + bf_ref[d, hh])
                b = jnp.dot(lf, cum, preferred_element_type=F32, precision=lax.Precision.HIGHEST)
                b_ref[bb * H_M + hh] = b
                e_ref[bb * H_M + hh] = ig - b

    cc = (n_chunks - 1 - c) if reverse else c

    def front(ch):
        bb, hh = divmod(ch, H_M)
        sl = slice(hh * DH_M, (hh + 1) * DH_M)
        b_row = b_ref[ch, pl.ds(cc, 1), :]
        e_row = e_ref[ch, pl.ds(cc, 1), :]
        b_col = jnp.sum(jnp.where(eye, b_row, 0.0), axis=1, keepdims=True)
        e_col = jnp.sum(jnp.where(eye, e_row, 0.0), axis=1, keepdims=True)
        a_tot = b_row[:, 0:1] if reverse else b_row[:, L - 1:L]
        m_prev = m_ref[ch]

        q = q_ref[bb, :, sl]
        kt = kt_refs[bb][sl, :]
        v = v_ref[bb, :, sl]

        dmat = jnp.where(tri, b_col + e_row, -jnp.inf)
        inter = b_col + m_prev
        m_t = jnp.maximum(inter, jnp.max(dmat, axis=1, keepdims=True))
        s = jnp.dot(q, kt, preferred_element_type=F32)
        ct = ct_ref[ch]
        inter_aug = jnp.dot(q, ct.astype(BF16), preferred_element_type=F32)
        return (e_col, a_tot, m_prev, kt, v, dmat, inter, m_t, s, ct, inter_aug)

    def back(ch, e_col, a_tot, m_prev, kt, v, dmat, inter, m_t, s, ct, inter_aug):
        bb, hh = divmod(ch, H_M)
        sl = slice(hh * DH_M, (hh + 1) * DH_M)
        dexp = jnp.exp(dmat - m_t)
        sqk = s * dexp
        sc = jnp.exp(inter - m_t)
        num = sc * inter_aug[:, :DH_M] + jnp.dot(sqk.astype(BF16), v, preferred_element_type=F32)
        den = sc * inter_aug[:, DH_M:] + jnp.sum(sqk, axis=1, keepdims=True)
        rden = 1.0 / jnp.maximum(jnp.abs(den), jnp.exp(-m_t))
        hdir = num * jnp.concatenate([rden] * (DH_M // LANES), axis=1)

        g_col = a_tot + e_col
        m_loc = jnp.max(g_col, axis=0, keepdims=True)
        m_new = jnp.maximum(a_tot + m_prev, m_loc)
        s_prev = jnp.exp(a_tot + m_prev - m_new)
        w_col = jnp.exp(g_col - m_new)
        vw_aug = jnp.concatenate(
            [(v.astype(F32) * w_col).astype(BF16),
             jnp.broadcast_to(w_col, (L, LANES)).astype(BF16)], axis=1)
        ct_ref[ch] = s_prev * ct + jnp.dot(kt, vw_aug, preferred_element_type=F32)
        m_ref[ch] = m_new

        if reverse:
            hm = _rms_rows(hf_ref[bb, :, sl] + hdir, gmh_ref[:, sl])
            o_ref[bb, :, sl] = (jax.nn.sigmoid(og_ref[bb, :, sl].astype(F32)) * hm).astype(BF16)
        else:
            o_ref[bb, :, sl] = hdir

    for ch in range(ML_BATCH * H_M):
        back(ch, *front(ch))


def _mlstm_sweep(u, kt, gates_t, b_igate, b_fgate, B, S, *, reverse, hf=None, g_mh=None):
    L = ML_CHUNK
    N = S // L
    d = 1 if reverse else 0
    qoff = 3 * W_A // W_M
    cidx = (lambda c: N - 1 - c) if reverse else (lambda c: c)
    smem = pl.BlockSpec(memory_space=pltpu.SMEM)
    nb = ML_BATCH
    assert B % nb == 0
    nch = nb * H_M
    kt_specs = [pl.BlockSpec((W_M, L), functools.partial(
        lambda b, c, r: (0, (nb * b + r) * N + cidx(c)), r=r)) for r in range(nb)]
    in_specs = [
        smem, smem,
        pl.BlockSpec((nb, L, W_M), lambda b, c: (b, cidx(c), qoff)),
        *kt_specs,
        pl.BlockSpec((nb, L, W_M), lambda b, c: (b, cidx(c), qoff + 1)),
        pl.BlockSpec((nb, H_M, N, L), lambda b, c: (b, d, 0, 0)),
        pl.BlockSpec((nb, H_M, N, L), lambda b, c: (b, 2 + d, 0, 0)),
    ]
    args = [b_igate, b_fgate, u, *([kt] * nb), u, gates_t, gates_t]
    if reverse:
        in_specs += [
            pl.BlockSpec((nb, L, W_M), lambda b, c: (b, cidx(c), 0)),
            pl.BlockSpec((nb, L, W_M), lambda b, c: (b, cidx(c), qoff + 2)),
            pl.BlockSpec((1, W_M), lambda b, c: (0, 0)),
        ]
        args += [hf, u, g_mh.reshape(1, W_M)]
        out_dtype = BF16
    else:
        out_dtype = F32
    blocks = nb * (_nbytes((L, W_M), BF16) * 4 + _nbytes((H_M, N, L), F32) * 2
                   + _nbytes((L, W_M), F32) * 2)
    scratch = (nch * _nbytes((DH_M, DH_M + LANES), F32) + 2 * nch * _nbytes((N, L), F32)
               + nch * 8 * LANES * 4)
    return pl.pallas_call(
        functools.partial(_mlstm_kernel, reverse=reverse, n_chunks=N),
        grid=(B // nb, N),
        in_specs=in_specs,
        out_specs=pl.BlockSpec((nb, L, W_M), lambda b, c: (b, cidx(c), 0)),
        out_shape=jax.ShapeDtypeStruct((B, S, W_M), out_dtype),
        scratch_shapes=[
            pltpu.VMEM((nch, DH_M, DH_M + LANES), F32),
            pltpu.VMEM((nch, 1, 1), F32),
            pltpu.VMEM((nch, N, L), F32),
            pltpu.VMEM((nch, N, L), F32),
        ],
        compiler_params=pltpu.CompilerParams(
            dimension_semantics=("arbitrary", "arbitrary"),
            vmem_limit_bytes=_vmem_limit(blocks, scratch)),
        name="mlstm_bwd" if reverse else "mlstm_fwd",
    )(*args)


def _outproj_kernel(x_ref, ya_ref, ym_ref, wa_ref, wm_ref, o_ref):
    o_ref[...] = (x_ref[...]
                  + jnp.dot(ya_ref[...], wa_ref[...], preferred_element_type=F32)
                  + jnp.dot(ym_ref[...], wm_ref[...], preferred_element_type=F32))


def _outproj(x, ya, ym, w_a, w_m):
    T, D = x.shape
    blocks = (_nbytes((TM, D), F32) * 2 + _nbytes((TM, W_A), BF16) * 2 + _nbytes((W_A, D), BF16) * 2)
    return pl.pallas_call(
        _outproj_kernel,
        grid=(T // TM,),
        in_specs=[
            pl.BlockSpec((TM, D), lambda i: (i, 0)),
            pl.BlockSpec((TM, W_A), lambda i: (i, 0)),
            pl.BlockSpec((TM, W_M), lambda i: (i, 0)),
            pl.BlockSpec((W_A, D), lambda i: (0, 0)),
            pl.BlockSpec((W_M, D), lambda i: (0, 0)),
        ],
        out_specs=pl.BlockSpec((TM, D), lambda i: (i, 0)),
        out_shape=jax.ShapeDtypeStruct((T, D), F32),
        compiler_params=pltpu.CompilerParams(
            dimension_semantics=("arbitrary",),
            vmem_limit_bytes=_vmem_limit(blocks, 0)),
        name="outproj",
    )(x, ya, ym, w_a, w_m)


def _ple_kernel(x_ref, g_ref, pe_ref, wg_ref, wp_ref, o_ref):
    x = x_ref[...]
    hn = _rms_rows(x, g_ref[...]).astype(BF16)
    gate = jax.nn.sigmoid(jnp.dot(hn, wg_ref[...], preferred_element_type=F32))
    proj = jnp.dot(pe_ref[...].astype(BF16), wp_ref[...], preferred_element_type=F32)
    o_ref[...] = x + gate * proj


def _ple(x, g, pe, w_gate, w_proj):
    T, D = x.shape
    P = pe.shape[1]
    blocks = (_nbytes((TM, D), F32) * 2 + _nbytes((TM, P), F32) + _nbytes((D, D), BF16)
              + _nbytes((P, D), BF16))
    return pl.pallas_call(
        _ple_kernel,
        grid=(T // TM,),
        in_specs=[
            pl.BlockSpec((TM, D), lambda i: (i, 0)),
            pl.BlockSpec((1, D), lambda i: (0, 0)),
            pl.BlockSpec((TM, P), lambda i: (i, 0)),
            pl.BlockSpec((D, D), lambda i: (0, 0)),
            pl.BlockSpec((P, D), lambda i: (0, 0)),
        ],
        out_specs=pl.BlockSpec((TM, D), lambda i: (i, 0)),
        out_shape=jax.ShapeDtypeStruct((T, D), F32),
        compiler_params=pltpu.CompilerParams(
            dimension_semantics=("arbitrary",),
            vmem_limit_bytes=_vmem_limit(blocks, 0)),
        name="ple",
    )(x, g.reshape(1, D), pe, w_gate, w_proj)


def _layer(x, pe, w):
    B, S, D = x.shape
    T = B * S
    x2 = x.reshape(T, D)
    x2 = _ffn(x2, w["g_ffn1"], w["w1g"], w["w1u"], w["w1d"])

    u, gates, kt = _inproj(x2, w["g_mix"], w["w_in_t"], w["w_in_gate_t"], w["g_qn"], w["g_kn"])
    u = u.reshape(B, S, U_MAIN)
    ya = _attention(u, w["attn_bias"], B, S)

    n_chunks = S // ML_CHUNK
    gates_t = jnp.transpose(gates[:, :N_GATES].reshape(B, S, N_GATES), (0, 2, 1))
    gates_t = gates_t.reshape(B, N_GATES, n_chunks, ML_CHUNK)
    hf = _mlstm_sweep(u, kt, gates_t, w["b_igate"], w["b_fgate"], B, S, reverse=False)
    ym = _mlstm_sweep(u, kt, gates_t, w["b_igate"], w["b_fgate"], B, S, reverse=True,
                      hf=hf, g_mh=w["g_mh"])

    x2 = _outproj(x2, ya.reshape(T, W_A), ym.reshape(T, W_M), w["w_out_a"], w["w_out_m"])
    x2 = _ffn(x2, w["g_ffn2"], w["w2g"], w["w2u"], w["w2d"])
    x2 = _ple(x2, w["g_ple"], pe.reshape(T, -1), w["w_ple_gate"], w["w_ple_proj"])
    return x2.reshape(B, S, D)


def kernel(x_prompt, x_sample, p_prompt, p_sample, g_ffn1, w_ffn1_gate, w_ffn1_up, w_ffn1_down, g_mix, w_in, b_igate, b_fgate, g_qn, g_kn, rpb, g_mh, w_out, g_ffn2, w_ffn2_gate, w_ffn2_up, w_ffn2_down, g_ple, w_ple_gate, w_ple_proj):
    depth = g_ffn1.shape[0]
    xs = [x_prompt, x_sample]
    ps = [p_prompt, p_sample]
    for i in range(depth):
        w_in_t = jnp.transpose(w_in[i]).astype(BF16)
        u_end = 3 * W_A + 4 * W_M
        gate_rows = jnp.pad(w_in_t[u_end:], ((0, LANES - N_GATES), (0, 0)))
        w = {
            "g_ffn1": g_ffn1[i],
            "w1g": w_ffn1_gate[i].astype(BF16), "w1u": w_ffn1_up[i].astype(BF16),
            "w1d": w_ffn1_down[i],
            "g_mix": g_mix[i],
            "w_in_t": w_in_t,
            "w_in_gate_t": gate_rows,
            "b_igate": b_igate[i], "b_fgate": b_fgate[i], "g_qn": g_qn[i], "g_kn": g_kn[i],
            "attn_bias": _attn_bias_table(rpb[i]), "g_mh": g_mh[i],
            "w_out_a": w_out[i, :W_A].astype(BF16), "w_out_m": w_out[i, W_A:].astype(BF16),
            "g_ffn2": g_ffn2[i],
            "w2g": w_ffn2_gate[i].astype(BF16), "w2u": w_ffn2_up[i].astype(BF16),
            "w2d": w_ffn2_down[i],
            "g_ple": g_ple[i], "w_ple_gate": w_ple_gate[i].astype(BF16),
            "w_ple_proj": w_ple_proj[i].astype(BF16),
        }
        xs = [_layer(x, p[i], w) for x, p in zip(xs, ps)]
    return (xs[0], xs[1])
```

```python
import functools

import numpy as np
import jax
import jax.numpy as jnp
from jax import lax
from jax.experimental import pallas as pl
from jax.experimental.pallas import tpu as pltpu

F32 = jnp.float32
BF16 = jnp.bfloat16

GRID_W = 64
WIN_H = 8
WIN_W = 16
H_A = 8
DH_A = 128
H_M = 4
DH_M = 256
W_A = H_A * DH_A
W_M = H_M * DH_M
EPS = 1e-6
N_GATES = 4 * H_M
U_MAIN = 3 * W_A + 3 * W_M

LANES = 128
V7X_VMEM_BYTES = 64 * 1024 * 1024

TM = 512
TM_FFN = 1024
TM_IN = 1024
TF = 512
TN_IN = 1024
ATT_RB = 4
ATT_WIN = 12
ATT_HEADS = 8
ML_CHUNK = 256
ML_BATCH = 1
NEG_BIG = -1e30


def _vmem_limit(block_bytes, scratch_bytes):
    est = 2 * block_bytes + scratch_bytes + 12 * 1024 * 1024
    return int(min(est, V7X_VMEM_BYTES - 6 * 1024 * 1024))


def _nbytes(shape, dtype):
    return int(np.prod(shape)) * jnp.dtype(dtype).itemsize


def _rms_rows(x, g):
    ms = jnp.mean(x * x, axis=-1, keepdims=True)
    return x * lax.rsqrt(ms + EPS) * g


def _ffn_kernel(x_ref, g_ref, wg_ref, wu_ref, wd_ref, o_ref, hn_ref):
    j = pl.program_id(1)

    def half_step(hn):
        a = jnp.dot(hn, wg_ref[...], preferred_element_type=F32)
        b = jnp.dot(hn, wu_ref[...], preferred_element_type=F32)
        h = (a * jax.nn.sigmoid(a) * (0.5 * b)).astype(BF16)
        return jnp.dot(h, wd_ref[...].astype(BF16), preferred_element_type=F32)

    @pl.when(j == 0)
    def _():
        x = x_ref[...]
        hn = _rms_rows(x, g_ref[...]).astype(BF16)
        hn_ref[...] = hn
        o_ref[...] = x + half_step(hn)

    @pl.when(j > 0)
    def _():
        o_ref[...] += half_step(hn_ref[...])


def _ffn(x, g, wg, wu, wd):
    T, D = x.shape
    F = wd.shape[0]
    tm = TM_FFN
    blocks = (_nbytes((tm, D), F32) * 2 + _nbytes((D, 2 * TF), BF16) + _nbytes((TF, D), F32))
    scratch = _nbytes((tm, D), BF16)
    return pl.pallas_call(
        _ffn_kernel,
        grid=(T // tm, F // TF),
        in_specs=[
            pl.BlockSpec((tm, D), lambda i, j: (i, 0)),
            pl.BlockSpec((1, D), lambda i, j: (0, 0)),
            pl.BlockSpec((D, TF), lambda i, j: (0, j)),
            pl.BlockSpec((D, TF), lambda i, j: (0, j)),
            pl.BlockSpec((TF, D), lambda i, j: (j, 0)),
        ],
        out_specs=pl.BlockSpec((tm, D), lambda i, j: (i, 0)),
        out_shape=jax.ShapeDtypeStruct((T, D), F32),
        scratch_shapes=[pltpu.VMEM((tm, D), BF16)],
        compiler_params=pltpu.CompilerParams(
            dimension_semantics=("arbitrary", "arbitrary"),
            vmem_limit_bytes=_vmem_limit(blocks, scratch)),
        name="ffn",
    )(x, g.reshape(1, D), wg, wu, wd)


_NT = (((1,), (1,)), ((), ()))


def _inproj_kernel(x_ref, g_ref, wt_ref, wgate_ref, gq_ref, gk_ref,
                   u_ref, gates_ref, kt_ref, hn_ref, *, n_u_tiles):
    j = pl.program_id(1)

    n_q_tiles = W_A // TN_IN
    n_qk_tiles = 2 * n_q_tiles

    def project(hn):
        return lax.dot_general(hn, wt_ref[...], _NT, preferred_element_type=F32)

    def store_qk_normed(acc, gain):
        for hh in range(TN_IN // DH_A):
            sl = slice(hh * DH_A, (hh + 1) * DH_A)
            u_ref[:, sl] = _rms_rows(acc[:, sl], gain).astype(BF16)

    @pl.when(j == 0)
    def _():
        hn = _rms_rows(x_ref[...], g_ref[...]).astype(BF16)
        hn_ref[...] = hn
        gates_ref[...] = lax.dot_general(hn, wgate_ref[...], _NT, preferred_element_type=F32)
        store_qk_normed(project(hn), gq_ref[...] * (DH_A ** -0.5))

    @pl.when((j > 0) & (j < n_qk_tiles))
    def _():
        gain = jnp.where(j < n_q_tiles, gq_ref[...] * (DH_A ** -0.5), gk_ref[...])
        store_qk_normed(project(hn_ref[...]), gain)

    @pl.when((j >= n_qk_tiles) & (j < n_u_tiles))
    def _():
        u_ref[...] = project(hn_ref[...]).astype(BF16)

    @pl.when(j == n_u_tiles)
    def _():
        kt = lax.dot_general(wt_ref[...], hn_ref[...], _NT, preferred_element_type=F32)
        kt_ref[...] = (kt * (DH_M ** -0.5)).astype(BF16)


def _inproj(x, g, w_t, w_gate_t, g_qn, g_kn):
    T, D = x.shape
    tm = TM_IN
    nu = U_MAIN // TN_IN
    assert W_M == TN_IN
    blocks = (_nbytes((tm, D), F32) + _nbytes((TN_IN, D), BF16) + _nbytes((LANES, D), BF16)
              + _nbytes((tm, TN_IN), BF16) + _nbytes((tm, LANES), F32) + _nbytes((W_M, tm), BF16))
    scratch = _nbytes((tm, D), BF16)
    last = nu - 1
    km_block = (3 * W_A + W_M) // TN_IN

    def w_row_block(j):
        return jnp.where(j < km_block, j, jnp.where(j < nu, j + 1, km_block))

    return pl.pallas_call(
        functools.partial(_inproj_kernel, n_u_tiles=nu),
        grid=(T // tm, nu + 1),
        in_specs=[
            pl.BlockSpec((tm, D), lambda i, j: (i, 0)),
            pl.BlockSpec((1, D), lambda i, j: (0, 0)),
            pl.BlockSpec((TN_IN, D), lambda i, j: (w_row_block(j), 0)),
            pl.BlockSpec((LANES, D), lambda i, j: (0, 0)),
            pl.BlockSpec((1, DH_A), lambda i, j: (0, 0)),
            pl.BlockSpec((1, DH_A), lambda i, j: (0, 0)),
        ],
        out_specs=[
            pl.BlockSpec((tm, TN_IN), lambda i, j: (i, jnp.minimum(j, last))),
            pl.BlockSpec((tm, LANES), lambda i, j: (i, 0)),
            pl.BlockSpec((W_M, tm), lambda i, j: (0, i)),
        ],
        out_shape=[
            jax.ShapeDtypeStruct((T, U_MAIN), BF16),
            jax.ShapeDtypeStruct((T, LANES), F32),
            jax.ShapeDtypeStruct((W_M, T), BF16),
        ],
        scratch_shapes=[pltpu.VMEM((tm, D), BF16)],
        compiler_params=pltpu.CompilerParams(
            dimension_semantics=("arbitrary", "arbitrary"),
            vmem_limit_bytes=_vmem_limit(blocks, scratch)),
        name="inproj",
    )(x, g.reshape(1, D), w_t, w_gate_t, g_qn.reshape(1, DH_A), g_kn.reshape(1, DH_A))


def _attn_bias_table(rpb):
    a = np.arange(ATT_RB)[:, None]
    ik = np.arange(ATT_WIN)[None, :]
    kh = WIN_H
    valid0 = ik < kh
    dr0 = ik - a + (WIN_H - 1)
    valid1 = (ik >= a) & (ik < a + kh)
    dr1 = ik - a + (WIN_H - 1) - kh // 2
    valid2 = (ik >= ATT_WIN - kh) & (ik < ATT_WIN)
    dr2 = (ik - ATT_WIN) - (a - ATT_RB) + (WIN_H - 1)
    valid = np.stack([np.broadcast_to(valid0, dr0.shape), valid1,
                      np.broadcast_to(valid2, dr0.shape)])
    dr = np.stack([dr0, dr1, dr2])

    c = np.arange(GRID_W)
    cs = np.clip(c - WIN_W // 2, 0, GRID_W - WIN_W)
    colmask = (c[None, :] >= cs[:, None]) & (c[None, :] < cs[:, None] + WIN_W)
    dc = np.clip(c[None, :] - c[:, None], -(WIN_W - 1), WIN_W - 1) + (WIN_W - 1)

    onehot = (dc[None] == np.arange(2 * WIN_W - 1)[:, None, None]).astype(np.float32)
    tiles = jnp.einsum("hrc,cqk->hqrk", rpb.astype(F32), jnp.asarray(onehot),
                       precision=lax.Precision.HIGHEST)
    tiles = jnp.where(jnp.asarray(colmask)[:, None, :], tiles, NEG_BIG)
    seq = tiles.reshape(H_A, GRID_W, (2 * WIN_H - 1) * GRID_W)
    classes = []
    for cls in range(3):
        rows = []
        for qa in range(ATT_RB):
            kk = np.nonzero(valid[cls, qa])[0]
            k0, k1 = int(kk[0]), int(kk[-1]) + 1
            d0 = int(dr[cls, qa, k0])
            assert np.array_equal(kk, np.arange(k0, k1))
            assert np.array_equal(dr[cls, qa, k0:k1], np.arange(d0, d0 + k1 - k0))
            strip = seq[:, :, d0 * GRID_W:(d0 + k1 - k0) * GRID_W]
            rows.append(jnp.pad(strip, ((0, 0), (0, 0), (k0 * GRID_W, (ATT_WIN - k1) * GRID_W)),
                                constant_values=NEG_BIG))
        classes.append(jnp.concatenate(rows, axis=1))
    return jnp.stack(classes)


def _attn_kernel(q_ref, k_ref, v_ref, bias_ref, o_ref, *, n_rows):
    i = pl.program_id(2)
    nb = n_rows // ATT_RB
    cls = jnp.where(i == 0, 0, jnp.where(i == nb - 1, 2, 1))

    def scores(hh):
        sl = slice(hh * DH_A, (hh + 1) * DH_A)
        q = q_ref[0, :, sl]
        kw = k_ref[0, :, sl]
        s = lax.dot_general(q, kw, (((1,), (1,)), ((), ())), preferred_element_type=F32)
        s = s + bias_ref[cls, hh]
        return s, jnp.max(s, axis=-1, keepdims=True)

    def weighted_values(hh, s, m):
        sl = slice(hh * DH_A, (hh + 1) * DH_A)
        p = jnp.exp(s - m)
        l = jnp.sum(p, axis=-1, keepdims=True)
        o = jnp.dot(p.astype(BF16), v_ref[0, :, sl], preferred_element_type=F32)
        o_ref[0, :, sl] = (o / l).astype(BF16)

    pending = None
    for hh in range(ATT_HEADS):
        cur = scores(hh)
        if pending is not None:
            weighted_values(*pending)
        pending = (hh, *cur)
    weighted_values(*pending)


def _attention(u, bias, B, S):
    R = S // GRID_W
    assert R % ATT_RB == 0 and R >= ATT_WIN and R // ATT_RB >= 3
    tq = ATT_RB * GRID_W
    tk = ATT_WIN * GRID_W
    hw = ATT_HEADS * DH_A
    ng = H_A // ATT_HEADS
    blocks = (_nbytes((tq, hw), BF16) * 2 + _nbytes((tk, hw), BF16) * 2
              + _nbytes((3, ATT_HEADS, tq, tk), F32))

    def win_start(i):
        return jnp.clip(i * ATT_RB - WIN_H // 2, 0, R - ATT_WIN) * GRID_W

    return pl.pallas_call(
        functools.partial(_attn_kernel, n_rows=R),
        grid=(ng, B, R // ATT_RB),
        in_specs=[
            pl.BlockSpec((1, tq, hw), lambda h, b, i: (b, i, h)),
            pl.BlockSpec((pl.Element(1), pl.Element(tk), pl.Element(hw)),
                         lambda h, b, i: (b, win_start(i), (ng + h) * hw)),
            pl.BlockSpec((pl.Element(1), pl.Element(tk), pl.Element(hw)),
                         lambda h, b, i: (b, win_start(i), (2 * ng + h) * hw)),
            pl.BlockSpec((3, ATT_HEADS, tq, tk), lambda h, b, i: (0, h, 0, 0)),
        ],
        out_specs=pl.BlockSpec((1, tq, hw), lambda h, b, i: (b, i, h)),
        out_shape=jax.ShapeDtypeStruct((B, S, W_A), BF16),
        compiler_params=pltpu.CompilerParams(
            dimension_semantics=("arbitrary", "arbitrary", "arbitrary"),
            vmem_limit_bytes=_vmem_limit(blocks, 0)),
        name="attention",
    )(u, u, u, bias)


def _mlstm_kernel(*refs, reverse, n_chunks):
    bi_ref, bf_ref, q_ref = refs[:3]
    kt_refs = refs[3:3 + ML_BATCH]
    rest = refs[3 + ML_BATCH:]
    if reverse:
        (v_ref, gi_ref, gf_ref, hf_ref, og_ref, gmh_ref,
         o_ref, ct_ref, m_ref, b_ref, e_ref) = rest
    else:
        (v_ref, gi_ref, gf_ref, o_ref, ct_ref, m_ref, b_ref, e_ref) = rest
    L = ML_CHUNK
    c = pl.program_id(1)
    d = 1 if reverse else 0

    row = lax.broadcasted_iota(jnp.int32, (L, L), 0)
    col = lax.broadcasted_iota(jnp.int32, (L, L), 1)
    tri = (col >= row) if reverse else (col <= row)
    eye = row == col

    @pl.when(c == 0)
    def _():
        ct_ref[...] = jnp.zeros_like(ct_ref)
        m_ref[...] = jnp.zeros_like(m_ref)
        cum = jnp.where((row >= col) if reverse else (row <= col), 1.0, 0.0).astype(F32)
        for bb in range(ML_BATCH):
            for hh in range(H_M):
                ig = gi_ref[bb, hh] + bi_ref[d, hh]
                lf = jax.nn.log_sigmoid(gf_ref[bb, hh] + bf_ref[d, hh])
                b = jnp.dot(lf, cum, preferred_element_type=F32, precision=lax.Precision.HIGHEST)
                b_ref[bb * H_M + hh] = b
                e_ref[bb * H_M + hh] = ig - b

    cc = (n_chunks - 1 - c) if reverse else c

    def chain(ch):
        bb, hh = divmod(ch, H_M)
        sl = slice(hh * DH_M, (hh + 1) * DH_M)
        b_row = b_ref[ch, pl.ds(cc, 1), :]
        e_row = e_ref[ch, pl.ds(cc, 1), :]
        b_col = jnp.sum(jnp.where(eye, b_row, 0.0), axis=1, keepdims=True)
        e_col = jnp.sum(jnp.where(eye, e_row, 0.0), axis=1, keepdims=True)
        a_tot = b_row[:, 0:1] if reverse else b_row[:, L - 1:L]
        m_prev = m_ref[ch]
        yield

        q = q_ref[bb, :, sl]
        kt = kt_refs[bb][sl, :]
        v = v_ref[bb, :, sl]

        dmat = jnp.where(tri, b_col + e_row, -jnp.inf)
        inter = b_col + m_prev
        m_t = jnp.maximum(inter, jnp.max(dmat, axis=1, keepdims=True))
        s = jnp.dot(q, kt, preferred_element_type=F32)
        ct = ct_ref[ch]
        inter_aug = jnp.dot(q, ct.astype(BF16), preferred_element_type=F32)
        yield

        dexp = jnp.exp(dmat - m_t)
        sqk = s * dexp
        sc = jnp.exp(inter - m_t)
        yield

        num = sc * inter_aug[:, :DH_M] + jnp.dot(sqk.astype(BF16), v, preferred_element_type=F32)
        den = sc * inter_aug[:, DH_M:] + jnp.sum(sqk, axis=1, keepdims=True)
        rden = 1.0 / jnp.maximum(jnp.abs(den), jnp.exp(-m_t))
        hdir = num * jnp.concatenate([rden] * (DH_M // LANES), axis=1)

        if reverse:
            hm = _rms_rows(hf_ref[bb, :, sl] + hdir, gmh_ref[:, sl])
            o_ref[bb, :, sl] = (jax.nn.sigmoid(og_ref[bb, :, sl].astype(F32)) * hm).astype(BF16)
        else:
            o_ref[bb, :, sl] = hdir
        yield

        g_col = a_tot + e_col
        m_loc = jnp.max(g_col, axis=0, keepdims=True)
        m_new = jnp.maximum(a_tot + m_prev, m_loc)
        s_prev = jnp.exp(a_tot + m_prev - m_new)
        w_col = jnp.exp(g_col - m_new)
        vw_aug = jnp.concatenate(
            [(v.astype(F32) * w_col).astype(BF16),
             jnp.broadcast_to(w_col, (L, LANES)).astype(BF16)], axis=1)
        ct_ref[ch] = s_prev * ct + jnp.dot(kt, vw_aug, preferred_element_type=F32)
        m_ref[ch] = m_new

    chains = [chain(ch) for ch in range(ML_BATCH * H_M)]
    while chains:
        chains = [g for g in chains if next(g, StopIteration) is not StopIteration]


def _mlstm_sweep(u, kt, gates_t, b_igate, b_fgate, B, S, *, reverse, hf=None, g_mh=None):
    L = ML_CHUNK
    N = S // L
    d = 1 if reverse else 0
    qoff = 3 * W_A // W_M
    cidx = (lambda c: N - 1 - c) if reverse else (lambda c: c)
    smem = pl.BlockSpec(memory_space=pltpu.SMEM)
    nb = ML_BATCH
    assert B % nb == 0
    nch = nb * H_M
    kt_specs = [pl.BlockSpec((W_M, L), functools.partial(
        lambda b, c, r: (0, (nb * b + r) * N + cidx(c)), r=r)) for r in range(nb)]
    in_specs = [
        smem, smem,
        pl.BlockSpec((nb, L, W_M), lambda b, c: (b, cidx(c), qoff)),
        *kt_specs,
        pl.BlockSpec((nb, L, W_M), lambda b, c: (b, cidx(c), qoff + 1)),
        pl.BlockSpec((nb, H_M, N, L), lambda b, c: (b, d, 0, 0)),
        pl.BlockSpec((nb, H_M, N, L), lambda b, c: (b, 2 + d, 0, 0)),
    ]
    args = [b_igate, b_fgate, u, *([kt] * nb), u, gates_t, gates_t]
    if reverse:
        in_specs += [
            pl.BlockSpec((nb, L, W_M), lambda b, c: (b, cidx(c), 0)),
            pl.BlockSpec((nb, L, W_M), lambda b, c: (b, cidx(c), qoff + 2)),
            pl.BlockSpec((1, W_M), lambda b, c: (0, 0)),
        ]
        args += [hf, u, g_mh.reshape(1, W_M)]
        out_dtype = BF16
    else:
        out_dtype = F32
    blocks = nb * (_nbytes((L, W_M), BF16) * 4 + _nbytes((H_M, N, L), F32) * 2
                   + _nbytes((L, W_M), F32) * 2)
    scratch = (nch * _nbytes((DH_M, DH_M + LANES), F32) + 2 * nch * _nbytes((N, L), F32)
               + nch * 8 * LANES * 4)
    return pl.pallas_call(
        functools.partial(_mlstm_kernel, reverse=reverse, n_chunks=N),
        grid=(B // nb, N),
        in_specs=in_specs,
        out_specs=pl.BlockSpec((nb, L, W_M), lambda b, c: (b, cidx(c), 0)),
        out_shape=jax.ShapeDtypeStruct((B, S, W_M), out_dtype),
        scratch_shapes=[
            pltpu.VMEM((nch, DH_M, DH_M + LANES), F32),
            pltpu.VMEM((nch, 1, 1), F32),
            pltpu.VMEM((nch, N, L), F32),
            pltpu.VMEM((nch, N, L), F32),
        ],
        compiler_params=pltpu.CompilerParams(
            dimension_semantics=("arbitrary", "arbitrary"),
            vmem_limit_bytes=_vmem_limit(blocks, scratch)),
        name="mlstm_bwd" if reverse else "mlstm_fwd",
    )(*args)


def _outproj_kernel(x_ref, ya_ref, ym_ref, wa_ref, wm_ref, o_ref):
    o_ref[...] = (x_ref[...]
                  + jnp.dot(ya_ref[...], wa_ref[...], preferred_element_type=F32)
                  + jnp.dot(ym_ref[...], wm_ref[...], preferred_element_type=F32))


def _outproj(x, ya, ym, w_a, w_m):
    T, D = x.shape
    blocks = (_nbytes((TM, D), F32) * 2 + _nbytes((TM, W_A), BF16) * 2 + _nbytes((W_A, D), BF16) * 2)
    return pl.pallas_call(
        _outproj_kernel,
        grid=(T // TM,),
        in_specs=[
            pl.BlockSpec((TM, D), lambda i: (i, 0)),
            pl.BlockSpec((TM, W_A), lambda i: (i, 0)),
            pl.BlockSpec((TM, W_M), lambda i: (i, 0)),
            pl.BlockSpec((W_A, D), lambda i: (0, 0)),
            pl.BlockSpec((W_M, D), lambda i: (0, 0)),
        ],
        out_specs=pl.BlockSpec((TM, D), lambda i: (i, 0)),
        out_shape=jax.ShapeDtypeStruct((T, D), F32),
        compiler_params=pltpu.CompilerParams(
            dimension_semantics=("arbitrary",),
            vmem_limit_bytes=_vmem_limit(blocks, 0)),
        name="outproj",
    )(x, ya, ym, w_a, w_m)


def _ple_kernel(x_ref, g_ref, pe_ref, wg_ref, wp_ref, o_ref):
    x = x_ref[...]
    hn = _rms_rows(x, g_ref[...]).astype(BF16)
    gate = jax.nn.sigmoid(jnp.dot(hn, wg_ref[...], preferred_element_type=F32))
    proj = jnp.dot(pe_ref[...].astype(BF16), wp_ref[...], preferred_element_type=F32)
    o_ref[...] = x + gate * proj


def _ple(x, g, pe, w_gate, w_proj):
    T, D = x.shape
    P = pe.shape[1]
    blocks = (_nbytes((TM, D), F32) * 2 + _nbytes((TM, P), F32) + _nbytes((D, D), BF16)
              + _nbytes((P, D), BF16))
    return pl.pallas_call(
        _ple_kernel,
        grid=(T // TM,),
        in_specs=[
            pl.BlockSpec((TM, D), lambda i: (i, 0)),
            pl.BlockSpec((1, D), lambda i: (0, 0)),
            pl.BlockSpec((TM, P), lambda i: (i, 0)),
            pl.BlockSpec((D, D), lambda i: (0, 0)),
            pl.BlockSpec((P, D), lambda i: (0, 0)),
        ],
        out_specs=pl.BlockSpec((TM, D), lambda i: (i, 0)),
        out_shape=jax.ShapeDtypeStruct((T, D), F32),
        compiler_params=pltpu.CompilerParams(
            dimension_semantics=("arbitrary",),
            vmem_limit_bytes=_vmem_limit(blocks, 0)),
        name="ple",
    )(x, g.reshape(1, D), pe, w_gate, w_proj)


def _layer(x, pe, w):
    B, S, D = x.shape
    T = B * S
    x2 = x.reshape(T, D)
    x2 = _ffn(x2, w["g_ffn1"], w["w1g"], w["w1u"], w["w1d"])

    u, gates, kt = _inproj(x2, w["g_mix"], w["w_in_t"], w["w_in_gate_t"], w["g_qn"], w["g_kn"])
    u = u.reshape(B, S, U_MAIN)
    ya = _attention(u, w["attn_bias"], B, S)

    n_chunks = S // ML_CHUNK
    gates_t = jnp.transpose(gates[:, :N_GATES].reshape(B, S, N_GATES), (0, 2, 1))
    gates_t = gates_t.reshape(B, N_GATES, n_chunks, ML_CHUNK)
    hf = _mlstm_sweep(u, kt, gates_t, w["b_igate"], w["b_fgate"], B, S, reverse=False)
    ym = _mlstm_sweep(u, kt, gates_t, w["b_igate"], w["b_fgate"], B, S, reverse=True,
                      hf=hf, g_mh=w["g_mh"])

    x2 = _outproj(x2, ya.reshape(T, W_A), ym.reshape(T, W_M), w["w_out_a"], w["w_out_m"])
    x2 = _ffn(x2, w["g_ffn2"], w["w2g"], w["w2u"], w["w2d"])
    x2 = _ple(x2, w["g_ple"], pe.reshape(T, -1), w["w_ple_gate"], w["w_ple_proj"])
    return x2.reshape(B, S, D)


def kernel(x_prompt, x_sample, p_prompt, p_sample, g_ffn1, w_ffn1_gate, w_ffn1_up, w_ffn1_down, g_mix, w_in, b_igate, b_fgate, g_qn, g_kn, rpb, g_mh, w_out, g_ffn2, w_ffn2_gate, w_ffn2_up, w_ffn2_down, g_ple, w_ple_gate, w_ple_proj):
    depth = g_ffn1.shape[0]
    xs = [x_prompt, x_sample]
    ps = [p_prompt, p_sample]
    for i in range(depth):
        w_in_t = jnp.transpose(w_in[i]).astype(BF16)
        u_end = 3 * W_A + 4 * W_M
        gate_rows = jnp.pad(w_in_t[u_end:], ((0, LANES - N_GATES), (0, 0)))
        w = {
            "g_ffn1": g_ffn1[i],
            "w1g": w_ffn1_gate[i].astype(BF16), "w1u": w_ffn1_up[i].astype(BF16),
            "w1d": w_ffn1_down[i],
            "g_mix": g_mix[i],
            "w_in_t": w_in_t,
            "w_in_gate_t": gate_rows,
            "b_igate": b_igate[i], "b_fgate": b_fgate[i], "g_qn": g_qn[i], "g_kn": g_kn[i],
            "attn_bias": _attn_bias_table(rpb[i]), "g_mh": g_mh[i],
            "w_out_a": w_out[i, :W_A].astype(BF16), "w_out_m": w_out[i, W_A:].astype(BF16),
            "g_ffn2": g_ffn2[i],
            "w2g": w_ffn2_gate[i].astype(BF16), "w2u": w_ffn2_up[i].astype(BF16),
            "w2d": w_ffn2_down[i],
            "g_ple": g_ple[i], "w_ple_gate": w_ple_gate[i].astype(BF16),
            "w_ple_proj": w_ple_proj[i].astype(BF16),
        }
        xs = [_layer(x, p[i], w) for x, p in zip(xs, ps)]
    return (xs[0], xs[1])
```

```python
import functools

import numpy as np
import jax
import jax.numpy as jnp
from jax import lax
from jax.experimental import pallas as pl
from jax.experimental.pallas import tpu as pltpu

F32 = jnp.float32
BF16 = jnp.bfloat16

GRID_W = 64
WIN_H = 8
WIN_W = 16
H_A = 8
DH_A = 128
H_M = 4
DH_M = 256
W_A = H_A * DH_A
W_M = H_M * DH_M
EPS = 1e-6
N_GATES = 4 * H_M
U_MAIN = 3 * W_A + 3 * W_M

LANES = 128
V7X_VMEM_BYTES = 64 * 1024 * 1024

TM = 512
TM_FFN = 1024
TM_IN = 1024
TF = 512
TN_IN = 1024
ATT_RB = 4
ATT_WIN = 12
ATT_HEADS = 8
ML_CHUNK = 256
ML_BATCH = 2
NEG_BIG = -1e30


def _vmem_limit(block_bytes, scratch_bytes):
    est = 2 * block_bytes + scratch_bytes + 12 * 1024 * 1024
    return int(min(est, V7X_VMEM_BYTES - 6 * 1024 * 1024))


def _nbytes(shape, dtype):
    return int(np.prod(shape)) * jnp.dtype(dtype).itemsize


def _rms_rows(x, g):
    ms = jnp.mean(x * x, axis=-1, keepdims=True)
    return x * lax.rsqrt(ms + EPS) * g


def _ffn_kernel(x_ref, g_ref, wg_ref, wu_ref, wd_ref, o_ref, hn_ref):
    j = pl.program_id(1)

    def half_step(hn):
        a = jnp.dot(hn, wg_ref[...], preferred_element_type=F32)
        b = jnp.dot(hn, wu_ref[...], preferred_element_type=F32)
        h = (a * jax.nn.sigmoid(a) * (0.5 * b)).astype(BF16)
        return jnp.dot(h, wd_ref[...].astype(BF16), preferred_element_type=F32)

    @pl.when(j == 0)
    def _():
        x = x_ref[...]
        hn = _rms_rows(x, g_ref[...]).astype(BF16)
        hn_ref[...] = hn
        o_ref[...] = x + half_step(hn)

    @pl.when(j > 0)
    def _():
        o_ref[...] += half_step(hn_ref[...])


def _ffn(x, g, wg, wu, wd):
    T, D = x.shape
    F = wd.shape[0]
    tm = TM_FFN
    blocks = (_nbytes((tm, D), F32) * 2 + _nbytes((D, 2 * TF), BF16) + _nbytes((TF, D), F32))
    scratch = _nbytes((tm, D), BF16)
    return pl.pallas_call(
        _ffn_kernel,
        grid=(T // tm, F // TF),
        in_specs=[
            pl.BlockSpec((tm, D), lambda i, j: (i, 0)),
            pl.BlockSpec((1, D), lambda i, j: (0, 0)),
            pl.BlockSpec((D, TF), lambda i, j: (0, j)),
            pl.BlockSpec((D, TF), lambda i, j: (0, j)),
            pl.BlockSpec((TF, D), lambda i, j: (j, 0)),
        ],
        out_specs=pl.BlockSpec((tm, D), lambda i, j: (i, 0)),
        out_shape=jax.ShapeDtypeStruct((T, D), F32),
        scratch_shapes=[pltpu.VMEM((tm, D), BF16)],
        compiler_params=pltpu.CompilerParams(
            dimension_semantics=("arbitrary", "arbitrary"),
            vmem_limit_bytes=_vmem_limit(blocks, scratch)),
        name="ffn",
    )(x, g.reshape(1, D), wg, wu, wd)


_NT = (((1,), (1,)), ((), ()))


def _inproj_kernel(x_ref, g_ref, wt_ref, wgate_ref, gq_ref, gk_ref,
                   u_ref, gates_ref, kt_ref, hn_ref, *, n_u_tiles):
    j = pl.program_id(1)

    n_q_tiles = W_A // TN_IN
    n_qk_tiles = 2 * n_q_tiles

    def project(hn):
        return lax.dot_general(hn, wt_ref[...], _NT, preferred_element_type=F32)

    def store_qk_normed(acc, gain):
        for hh in range(TN_IN // DH_A):
            sl = slice(hh * DH_A, (hh + 1) * DH_A)
            u_ref[:, sl] = _rms_rows(acc[:, sl], gain).astype(BF16)

    @pl.when(j == 0)
    def _():
        hn = _rms_rows(x_ref[...], g_ref[...]).astype(BF16)
        hn_ref[...] = hn
        gates_ref[...] = lax.dot_general(hn, wgate_ref[...], _NT, preferred_element_type=F32)
        store_qk_normed(project(hn), gq_ref[...] * (DH_A ** -0.5))

    @pl.when((j > 0) & (j < n_qk_tiles))
    def _():
        gain = jnp.where(j < n_q_tiles, gq_ref[...] * (DH_A ** -0.5), gk_ref[...])
        store_qk_normed(project(hn_ref[...]), gain)

    @pl.when((j >= n_qk_tiles) & (j < n_u_tiles))
    def _():
        u_ref[...] = project(hn_ref[...]).astype(BF16)

    @pl.when(j == n_u_tiles)
    def _():
        kt = lax.dot_general(wt_ref[...], hn_ref[...], _NT, preferred_element_type=F32)
        kt_ref[...] = (kt * (DH_M ** -0.5)).astype(BF16)


def _inproj(x, g, w_t, w_gate_t, g_qn, g_kn):
    T, D = x.shape
    tm = TM_IN
    nu = U_MAIN // TN_IN
    assert W_M == TN_IN
    blocks = (_nbytes((tm, D), F32) + _nbytes((TN_IN, D), BF16) + _nbytes((LANES, D), BF16)
              + _nbytes((tm, TN_IN), BF16) + _nbytes((tm, LANES), F32) + _nbytes((W_M, tm), BF16))
    scratch = _nbytes((tm, D), BF16)
    last = nu - 1
    km_block = (3 * W_A + W_M) // TN_IN

    def w_row_block(j):
        return jnp.where(j < km_block, j, jnp.where(j < nu, j + 1, km_block))

    return pl.pallas_call(
        functools.partial(_inproj_kernel, n_u_tiles=nu),
        grid=(T // tm, nu + 1),
        in_specs=[
            pl.BlockSpec((tm, D), lambda i, j: (i, 0)),
            pl.BlockSpec((1, D), lambda i, j: (0, 0)),
            pl.BlockSpec((TN_IN, D), lambda i, j: (w_row_block(j), 0)),
            pl.BlockSpec((LANES, D), lambda i, j: (0, 0)),
            pl.BlockSpec((1, DH_A), lambda i, j: (0, 0)),
            pl.BlockSpec((1, DH_A), lambda i, j: (0, 0)),
        ],
        out_specs=[
            pl.BlockSpec((tm, TN_IN), lambda i, j: (i, jnp.minimum(j, last))),
            pl.BlockSpec((tm, LANES), lambda i, j: (i, 0)),
            pl.BlockSpec((W_M, tm), lambda i, j: (0, i)),
        ],
        out_shape=[
            jax.ShapeDtypeStruct((T, U_MAIN), BF16),
            jax.ShapeDtypeStruct((T, LANES), F32),
            jax.ShapeDtypeStruct((W_M, T), BF16),
        ],
        scratch_shapes=[pltpu.VMEM((tm, D), BF16)],
        compiler_params=pltpu.CompilerParams(
            dimension_semantics=("arbitrary", "arbitrary"),
            vmem_limit_bytes=_vmem_limit(blocks, scratch)),
        name="inproj",
    )(x, g.reshape(1, D), w_t, w_gate_t, g_qn.reshape(1, DH_A), g_kn.reshape(1, DH_A))


def _attn_bias_table(rpb):
    a = np.arange(ATT_RB)[:, None]
    ik = np.arange(ATT_WIN)[None, :]
    kh = WIN_H
    valid0 = ik < kh
    dr0 = ik - a + (WIN_H - 1)
    valid1 = (ik >= a) & (ik < a + kh)
    dr1 = ik - a + (WIN_H - 1) - kh // 2
    valid2 = (ik >= ATT_WIN - kh) & (ik < ATT_WIN)
    dr2 = (ik - ATT_WIN) - (a - ATT_RB) + (WIN_H - 1)
    valid = np.stack([np.broadcast_to(valid0, dr0.shape), valid1,
                      np.broadcast_to(valid2, dr0.shape)])
    dr = np.stack([dr0, dr1, dr2])

    c = np.arange(GRID_W)
    cs = np.clip(c - WIN_W // 2, 0, GRID_W - WIN_W)
    colmask = (c[None, :] >= cs[:, None]) & (c[None, :] < cs[:, None] + WIN_W)
    dc = np.clip(c[None, :] - c[:, None], -(WIN_W - 1), WIN_W - 1) + (WIN_W - 1)

    onehot = (dc[None] == np.arange(2 * WIN_W - 1)[:, None, None]).astype(np.float32)
    tiles = jnp.einsum("hrc,cqk->hqrk", rpb.astype(F32), jnp.asarray(onehot),
                       precision=lax.Precision.HIGHEST)
    tiles = jnp.where(jnp.asarray(colmask)[:, None, :], tiles, NEG_BIG)
    seq = tiles.reshape(H_A, GRID_W, (2 * WIN_H - 1) * GRID_W)
    classes = []
    for cls in range(3):
        rows = []
        for qa in range(ATT_RB):
            kk = np.nonzero(valid[cls, qa])[0]
            k0, k1 = int(kk[0]), int(kk[-1]) + 1
            d0 = int(dr[cls, qa, k0])
            assert np.array_equal(kk, np.arange(k0, k1))
            assert np.array_equal(dr[cls, qa, k0:k1], np.arange(d0, d0 + k1 - k0))
            strip = seq[:, :, d0 * GRID_W:(d0 + k1 - k0) * GRID_W]
            rows.append(jnp.pad(strip, ((0, 0), (0, 0), (k0 * GRID_W, (ATT_WIN - k1) * GRID_W)),
                                constant_values=NEG_BIG))
        classes.append(jnp.concatenate(rows, axis=1))
    return jnp.stack(classes)


def _attn_kernel(q_ref, k_ref, v_ref, bias_ref, o_ref, *, n_rows):
    i = pl.program_id(2)
    nb = n_rows // ATT_RB
    cls = jnp.where(i == 0, 0, jnp.where(i == nb - 1, 2, 1))

    def scores(hh):
        sl = slice(hh * DH_A, (hh + 1) * DH_A)
        q = q_ref[0, :, sl]
        kw = k_ref[0, :, sl]
        s = lax.dot_general(q, kw, (((1,), (1,)), ((), ())), preferred_element_type=F32)
        s = s + bias_ref[cls, hh]
        return s, jnp.max(s, axis=-1, keepdims=True)

    def weighted_values(hh, s, m):
        sl = slice(hh * DH_A, (hh + 1) * DH_A)
        p = jnp.exp(s - m)
        l = jnp.sum(p, axis=-1, keepdims=True)
        o = jnp.dot(p.astype(BF16), v_ref[0, :, sl], preferred_element_type=F32)
        o_ref[0, :, sl] = (o / l).astype(BF16)

    pending = None
    for hh in range(ATT_HEADS):
        cur = scores(hh)
        if pending is not None:
            weighted_values(*pending)
        pending = (hh, *cur)
    weighted_values(*pending)


def _attention(u, bias, B, S):
    R = S // GRID_W
    assert R % ATT_RB == 0 and R >= ATT_WIN and R // ATT_RB >= 3
    tq = ATT_RB * GRID_W
    tk = ATT_WIN * GRID_W
    hw = ATT_HEADS * DH_A
    ng = H_A // ATT_HEADS
    blocks = (_nbytes((tq, hw), BF16) * 2 + _nbytes((tk, hw), BF16) * 2
              + _nbytes((3, ATT_HEADS, tq, tk), F32))

    def win_start(i):
        return jnp.clip(i * ATT_RB - WIN_H // 2, 0, R - ATT_WIN) * GRID_W

    return pl.pallas_call(
        functools.partial(_attn_kernel, n_rows=R),
        grid=(ng, B, R // ATT_RB),
        in_specs=[
            pl.BlockSpec((1, tq, hw), lambda h, b, i: (b, i, h)),
            pl.BlockSpec((pl.Element(1), pl.Element(tk), pl.Element(hw)),
                         lambda h, b, i: (b, win_start(i), (ng + h) * hw)),
            pl.BlockSpec((pl.Element(1), pl.Element(tk), pl.Element(hw)),
                         lambda h, b, i: (b, win_start(i), (2 * ng + h) * hw)),
            pl.BlockSpec((3, ATT_HEADS, tq, tk), lambda h, b, i: (0, h, 0, 0)),
        ],
        out_specs=pl.BlockSpec((1, tq, hw), lambda h, b, i: (b, i, h)),
        out_shape=jax.ShapeDtypeStruct((B, S, W_A), BF16),
        compiler_params=pltpu.CompilerParams(
            dimension_semantics=("arbitrary", "arbitrary", "arbitrary"),
            vmem_limit_bytes=_vmem_limit(blocks, 0)),
        name="attention",
    )(u, u, u, bias)


def _mlstm_kernel(*refs, reverse, n_chunks):
    bi_ref, bf_ref, q_ref = refs[:3]
    kt_refs = refs[3:3 + ML_BATCH]
    rest = refs[3 + ML_BATCH:]
    if reverse:
        (v_ref, gi_ref, gf_ref, hf_ref, og_ref, gmh_ref,
         o_ref, ct_ref, m_ref, b_ref, e_ref) = rest
    else:
        (v_ref, gi_ref, gf_ref, o_ref, ct_ref, m_ref, b_ref, e_ref) = rest
    L = ML_CHUNK
    c = pl.program_id(1)
    d = 1 if reverse else 0

    row = lax.broadcasted_iota(jnp.int32, (L, L), 0)
    col = lax.broadcasted_iota(jnp.int32, (L, L), 1)
    tri = (col >= row) if reverse else (col <= row)
    eye = row == col

    @pl.when(c == 0)
    def _():
        ct_ref[...] = jnp.zeros_like(ct_ref)
        m_ref[...] = jnp.zeros_like(m_ref)
        cum = jnp.where((row >= col) if reverse else (row <= col), 1.0, 0.0).astype(F32)
        for bb in range(ML_BATCH):
            for hh in range(H_M):
                ig = gi_ref[bb, hh] + bi_ref[d, hh]
                lf = jax.nn.log_sigmoid(gf_ref[bb, hh] + bf_ref[d, hh])
                b = jnp.dot(lf, cum, preferred_element_type=F32, precision=lax.Precision.HIGHEST)
                b_ref[bb * H_M + hh] = b
                e_ref[bb * H_M + hh] = ig - b

    cc = (n_chunks - 1 - c) if reverse else c

    def chain(ch):
        bb, hh = divmod(ch, H_M)
        sl = slice(hh * DH_M, (hh + 1) * DH_M)
        b_row = b_ref[ch, pl.ds(cc, 1), :]
        e_row = e_ref[ch, pl.ds(cc, 1), :]
        b_col = jnp.sum(jnp.where(eye, b_row, 0.0), axis=1, keepdims=True)
        e_col = jnp.sum(jnp.where(eye, e_row, 0.0), axis=1, keepdims=True)
        a_tot = b_row[:, 0:1] if reverse else b_row[:, L - 1:L]
        m_prev = m_ref[ch]
        yield

        q = q_ref[bb, :, sl]
        kt = kt_refs[bb][sl, :]
        v = v_ref[bb, :, sl]

        dmat = jnp.where(tri, b_col + e_row, -jnp.inf)
        inter = b_col + m_prev
        m_t = jnp.maximum(inter, jnp.max(dmat, axis=1, keepdims=True))
        s = jnp.dot(q, kt, preferred_element_type=F32)
        ct = ct_ref[ch]
        inter_aug = jnp.dot(q, ct.astype(BF16), preferred_element_type=F32)
        yield

        dexp = jnp.exp(dmat - m_t)
        sqk = s * dexp
        sc = jnp.exp(inter - m_t)
        yield

        num = sc * inter_aug[:, :DH_M] + jnp.dot(sqk.astype(BF16), v, preferred_element_type=F32)
        den = sc * inter_aug[:, DH_M:] + jnp.sum(sqk, axis=1, keepdims=True)
        rden = 1.0 / jnp.maximum(jnp.abs(den), jnp.exp(-m_t))
        hdir = num * jnp.concatenate([rden] * (DH_M // LANES), axis=1)

        if reverse:
            hm = _rms_rows(hf_ref[bb, :, sl] + hdir, gmh_ref[:, sl])
            o_ref[bb, :, sl] = (jax.nn.sigmoid(og_ref[bb, :, sl].astype(F32)) * hm).astype(BF16)
        else:
            o_ref[bb, :, sl] = hdir
        yield

        g_col = a_tot + e_col
        m_loc = jnp.max(g_col, axis=0, keepdims=True)
        m_new = jnp.maximum(a_tot + m_prev, m_loc)
        s_prev = jnp.exp(a_tot + m_prev - m_new)
        w_col = jnp.exp(g_col - m_new)
        vw_aug = jnp.concatenate(
            [(v.astype(F32) * w_col).astype(BF16),
             jnp.broadcast_to(w_col, (L, LANES)).astype(BF16)], axis=1)
        ct_ref[ch] = s_prev * ct + jnp.dot(kt, vw_aug, preferred_element_type=F32)
        m_ref[ch] = m_new

    chains = [chain(ch) for ch in range(ML_BATCH * H_M)]
    while chains:
        chains = [g for g in chains if next(g, StopIteration) is not StopIteration]


def _mlstm_sweep(u, kt, gates_t, b_igate, b_fgate, B, S, *, reverse, hf=None, g_mh=None):
    L = ML_CHUNK
    N = S // L
    d = 1 if reverse else 0
    qoff = 3 * W_A // W_M
    cidx = (lambda c: N - 1 - c) if reverse else (lambda c: c)
    smem = pl.BlockSpec(memory_space=pltpu.SMEM)
    nb = ML_BATCH
    assert B % nb == 0
    nch = nb * H_M
    kt_specs = [pl.BlockSpec((W_M, L), functools.partial(
        lambda b, c, r: (0, (nb * b + r) * N + cidx(c)), r=r)) for r in range(nb)]
    in_specs = [
        smem, smem,
        pl.BlockSpec((nb, L, W_M), lambda b, c: (b, cidx(c), qoff)),
        *kt_specs,
        pl.BlockSpec((nb, L, W_M), lambda b, c: (b, cidx(c), qoff + 1)),
        pl.BlockSpec((nb, H_M, N, L), lambda b, c: (b, d, 0, 0)),
        pl.BlockSpec((nb, H_M, N, L), lambda b, c: (b, 2 + d, 0, 0)),
    ]
    args = [b_igate, b_fgate, u, *([kt] * nb), u, gates_t, gates_t]
    if reverse:
        in_specs += [
            pl.BlockSpec((nb, L, W_M), lambda b, c: (b, cidx(c), 0)),
            pl.BlockSpec((nb, L, W_M), lambda b, c: (b, cidx(c), qoff + 2)),
            pl.BlockSpec((1, W_M), lambda b, c: (0, 0)),
        ]
        args += [hf, u, g_mh.reshape(1, W_M)]
        out_dtype = BF16
    else:
        out_dtype = F32
    blocks = nb * (_nbytes((L, W_M), BF16) * 4 + _nbytes((H_M, N, L), F32) * 2
                   + _nbytes((L, W_M), F32) * 2)
    scratch = (nch * _nbytes((DH_M, DH_M + LANES), F32) + 2 * nch * _nbytes((N, L), F32)
               + nch * 8 * LANES * 4)
    return pl.pallas_call(
        functools.partial(_mlstm_kernel, reverse=reverse, n_chunks=N),
        grid=(B // nb, N),
        in_specs=in_specs,
        out_specs=pl.BlockSpec((nb, L, W_M), lambda b, c: (b, cidx(c), 0)),
        out_shape=jax.ShapeDtypeStruct((B, S, W_M), out_dtype),
        scratch_shapes=[
            pltpu.VMEM((nch, DH_M, DH_M + LANES), F32),
            pltpu.VMEM((nch, 1, 1), F32),
            pltpu.VMEM((nch, N, L), F32),
            pltpu.VMEM((nch, N, L), F32),
        ],
        compiler_params=pltpu.CompilerParams(
            dimension_semantics=("arbitrary", "arbitrary"),
            vmem_limit_bytes=_vmem_limit(blocks, scratch)),
        name="mlstm_bwd" if reverse else "mlstm_fwd",
    )(*args)


def _outproj_kernel(x_ref, ya_ref, ym_ref, wa_ref, wm_ref, o_ref):
    o_ref[...] = (x_ref[...]
                  + jnp.dot(ya_ref[...], wa_ref[...], preferred_element_type=F32)
                  + jnp.dot(ym_ref[...], wm_ref[...], preferred_element_type=F32))


def _outproj(x, ya, ym, w_a, w_m):
    T, D = x.shape
    blocks = (_nbytes((TM, D), F32) * 2 + _nbytes((TM, W_A), BF16) * 2 + _nbytes((W_A, D), BF16) * 2)
    return pl.pallas_call(
        _outproj_kernel,
        grid=(T // TM,),
        in_specs=[
            pl.BlockSpec((TM, D), lambda i: (i, 0)),
            pl.BlockSpec((TM, W_A), lambda i: (i, 0)),
            pl.BlockSpec((TM, W_M), lambda i: (i, 0)),
            pl.BlockSpec((W_A, D), lambda i: (0, 0)),
            pl.BlockSpec((W_M, D), lambda i: (0, 0)),
        ],
        out_specs=pl.BlockSpec((TM, D), lambda i: (i, 0)),
        out_shape=jax.ShapeDtypeStruct((T, D), F32),
        compiler_params=pltpu.CompilerParams(
            dimension_semantics=("arbitrary",),
            vmem_limit_bytes=_vmem_limit(blocks, 0)),
        name="outproj",
    )(x, ya, ym, w_a, w_m)


def _ple_kernel(x_ref, g_ref, pe_ref, wg_ref, wp_ref, o_ref):
    x = x_ref[...]
    hn = _rms_rows(x, g_ref[...]).astype(BF16)
    gate = jax.nn.sigmoid(jnp.dot(hn, wg_ref[...], preferred_element_type=F32))
    proj = jnp.dot(pe_ref[...].astype(BF16), wp_ref[...], preferred_element_type=F32)
    o_ref[...] = x + gate * proj


def _ple(x, g, pe, w_gate, w_proj):
    T, D = x.shape
    P = pe.shape[1]
    blocks = (_nbytes((TM, D), F32) * 2 + _nbytes((TM, P), F32) + _nbytes((D, D), BF16)
              + _nbytes((P, D), BF16))
    return pl.pallas_call(
        _ple_kernel,
        grid=(T // TM,),
        in_specs=[
            pl.BlockSpec((TM, D), lambda i: (i, 0)),
            pl.BlockSpec((1, D), lambda i: (0, 0)),
            pl.BlockSpec((TM, P), lambda i: (i, 0)),
            pl.BlockSpec((D, D), lambda i: (0, 0)),
            pl.BlockSpec((P, D), lambda i: (0, 0)),
        ],
        out_specs=pl.BlockSpec((TM, D), lambda i: (i, 0)),
        out_shape=jax.ShapeDtypeStruct((T, D), F32),
        compiler_params=pltpu.CompilerParams(
            dimension_semantics=("arbitrary",),
            vmem_limit_bytes=_vmem_limit(blocks, 0)),
        name="ple",
    )(x, g.reshape(1, D), pe, w_gate, w_proj)


def _layer(x, pe, w):
    B, S, D = x.shape
    T = B * S
    x2 = x.reshape(T, D)
    x2 = _ffn(x2, w["g_ffn1"], w["w1g"], w["w1u"], w["w1d"])

    u, gates, kt = _inproj(x2, w["g_mix"], w["w_in_t"], w["w_in_gate_t"], w["g_qn"], w["g_kn"])
    u = u.reshape(B, S, U_MAIN)
    ya = _attention(u, w["attn_bias"], B, S)

    n_chunks = S // ML_CHUNK
    gates_t = jnp.transpose(gates[:, :N_GATES].reshape(B, S, N_GATES), (0, 2, 1))
    gates_t = gates_t.reshape(B, N_GATES, n_chunks, ML_CHUNK)
    hf = _mlstm_sweep(u, kt, gates_t, w["b_igate"], w["b_fgate"], B, S, reverse=False)
    ym = _mlstm_sweep(u, kt, gates_t, w["b_igate"], w["b_fgate"], B, S, reverse=True,
                      hf=hf, g_mh=w["g_mh"])

    x2 = _outproj(x2, ya.reshape(T, W_A), ym.reshape(T, W_M), w["w_out_a"], w["w_out_m"])
    x2 = _ffn(x2, w["g_ffn2"], w["w2g"], w["w2u"], w["w2d"])
    x2 = _ple(x2, w["g_ple"], pe.reshape(T, -1), w["w_ple_gate"], w["w_ple_proj"])
    return x2.reshape(B, S, D)


def kernel(x_prompt, x_sample, p_prompt, p_sample, g_ffn1, w_ffn1_gate, w_ffn1_up, w_ffn1_down, g_mix, w_in, b_igate, b_fgate, g_qn, g_kn, rpb, g_mh, w_out, g_ffn2, w_ffn2_gate, w_ffn2_up, w_ffn2_down, g_ple, w_ple_gate, w_ple_proj):
    depth = g_ffn1.shape[0]
    xs = [x_prompt, x_sample]
    ps = [p_prompt, p_sample]
    for i in range(depth):
        w_in_t = jnp.transpose(w_in[i]).astype(BF16)
        u_end = 3 * W_A + 4 * W_M
        gate_rows = jnp.pad(w_in_t[u_end:], ((0, LANES - N_GATES), (0, 0)))
        w = {
            "g_ffn1": g_ffn1[i],
            "w1g": w_ffn1_gate[i].astype(BF16), "w1u": w_ffn1_up[i].astype(BF16),
            "w1d": w_ffn1_down[i],
            "g_mix": g_mix[i],
            "w_in_t": w_in_t,
            "w_in_gate_t": gate_rows,
            "b_igate": b_igate[i], "b_fgate": b_fgate[i], "g_qn": g_qn[i], "g_kn": g_kn[i],
            "attn_bias": _attn_bias_table(rpb[i]), "g_mh": g_mh[i],
            "w_out_a": w_out[i, :W_A].astype(BF16), "w_out_m": w_out[i, W_A:].astype(BF16),
            "g_ffn2": g_ffn2[i],
            "w2g": w_ffn2_gate[i].astype(BF16), "w2u": w_ffn2_up[i].astype(BF16),
            "w2d": w_ffn2_down[i],
            "g_ple": g_ple[i], "w_ple_gate": w_ple_gate[i].astype(BF16),
            "w_ple_proj": w_ple_proj[i].astype(BF16),
        }
        xs = [_layer(x, p[i], w) for x, p in zip(xs, ps)]
    return (xs[0], xs[1])
```

```python
import functools

import numpy as np
import jax
import jax.numpy as jnp
from jax import lax
from jax.experimental import pallas as pl
from jax.experimental.pallas import tpu as pltpu

F32 = jnp.float32
BF16 = jnp.bfloat16

GRID_W = 64
WIN_H = 8
WIN_W = 16
H_A = 8
DH_A = 128
H_M = 4
DH_M = 256
W_A = H_A * DH_A
W_M = H_M * DH_M
EPS = 1e-6
N_GATES = 4 * H_M
U_MAIN = 3 * W_A + 3 * W_M

LANES = 128
V7X_VMEM_BYTES = 64 * 1024 * 1024

TM = 512
TM_FFN = 1024
TM_IN = 1024
TF = 512
TN_IN = 1024
ATT_RB = 4
ATT_WIN = 12
ATT_HEADS = 8
ATT_BLOCKS = 2
ML_CHUNK = 256
ML_BATCH = 2
NEG_BIG = -1e30


def _vmem_limit(block_bytes, scratch_bytes):
    est = 2 * block_bytes + scratch_bytes + 12 * 1024 * 1024
    return int(min(est, V7X_VMEM_BYTES - 6 * 1024 * 1024))


def _nbytes(shape, dtype):
    return int(np.prod(shape)) * jnp.dtype(dtype).itemsize


def _rms_rows(x, g):
    ms = jnp.mean(x * x, axis=-1, keepdims=True)
    return x * lax.rsqrt(ms + EPS) * g


def _ffn_kernel(x_ref, g_ref, wg_ref, wu_ref, wd_ref, o_ref, hn_ref):
    j = pl.program_id(1)

    def half_step(hn):
        a = jnp.dot(hn, wg_ref[...], preferred_element_type=F32)
        b = jnp.dot(hn, wu_ref[...], preferred_element_type=F32)
        h = (a * jax.nn.sigmoid(a) * (0.5 * b)).astype(BF16)
        return jnp.dot(h, wd_ref[...].astype(BF16), preferred_element_type=F32)

    @pl.when(j == 0)
    def _():
        x = x_ref[...]
        hn = _rms_rows(x, g_ref[...]).astype(BF16)
        hn_ref[...] = hn
        o_ref[...] = x + half_step(hn)

    @pl.when(j > 0)
    def _():
        o_ref[...] += half_step(hn_ref[...])


def _ffn(x, g, wg, wu, wd):
    T, D = x.shape
    F = wd.shape[0]
    tm = TM_FFN
    blocks = (_nbytes((tm, D), F32) * 2 + _nbytes((D, 2 * TF), BF16) + _nbytes((TF, D), F32))
    scratch = _nbytes((tm, D), BF16)
    return pl.pallas_call(
        _ffn_kernel,
        grid=(T // tm, F // TF),
        in_specs=[
            pl.BlockSpec((tm, D), lambda i, j: (i, 0)),
            pl.BlockSpec((1, D), lambda i, j: (0, 0)),
            pl.BlockSpec((D, TF), lambda i, j: (0, j)),
            pl.BlockSpec((D, TF), lambda i, j: (0, j)),
            pl.BlockSpec((TF, D), lambda i, j: (j, 0)),
        ],
        out_specs=pl.BlockSpec((tm, D), lambda i, j: (i, 0)),
        out_shape=jax.ShapeDtypeStruct((T, D), F32),
        scratch_shapes=[pltpu.VMEM((tm, D), BF16)],
        compiler_params=pltpu.CompilerParams(
            dimension_semantics=("arbitrary", "arbitrary"),
            vmem_limit_bytes=_vmem_limit(blocks, scratch)),
        name="ffn",
    )(x, g.reshape(1, D), wg, wu, wd)


_NT = (((1,), (1,)), ((), ()))


def _inproj_kernel(x_ref, g_ref, wt_ref, wgate_ref, gq_ref, gk_ref,
                   u_ref, gates_ref, kt_ref, hn_ref, *, n_u_tiles):
    j = pl.program_id(1)

    n_q_tiles = W_A // TN_IN
    n_qk_tiles = 2 * n_q_tiles

    def project(hn):
        return lax.dot_general(hn, wt_ref[...], _NT, preferred_element_type=F32)

    def store_qk_normed(acc, gain):
        for hh in range(TN_IN // DH_A):
            sl = slice(hh * DH_A, (hh + 1) * DH_A)
            u_ref[:, sl] = _rms_rows(acc[:, sl], gain).astype(BF16)

    @pl.when(j == 0)
    def _():
        hn = _rms_rows(x_ref[...], g_ref[...]).astype(BF16)
        hn_ref[...] = hn
        gates_ref[...] = lax.dot_general(hn, wgate_ref[...], _NT, preferred_element_type=F32)
        store_qk_normed(project(hn), gq_ref[...] * (DH_A ** -0.5))

    @pl.when((j > 0) & (j < n_qk_tiles))
    def _():
        gain = jnp.where(j < n_q_tiles, gq_ref[...] * (DH_A ** -0.5), gk_ref[...])
        store_qk_normed(project(hn_ref[...]), gain)

    @pl.when((j >= n_qk_tiles) & (j < n_u_tiles))
    def _():
        u_ref[...] = project(hn_ref[...]).astype(BF16)

    @pl.when(j == n_u_tiles)
    def _():
        kt = lax.dot_general(wt_ref[...], hn_ref[...], _NT, preferred_element_type=F32)
        kt_ref[...] = (kt * (DH_M ** -0.5)).astype(BF16)


def _inproj(x, g, w_t, w_gate_t, g_qn, g_kn):
    T, D = x.shape
    tm = TM_IN
    nu = U_MAIN // TN_IN
    assert W_M == TN_IN
    blocks = (_nbytes((tm, D), F32) + _nbytes((TN_IN, D), BF16) + _nbytes((LANES, D), BF16)
              + _nbytes((tm, TN_IN), BF16) + _nbytes((tm, LANES), F32) + _nbytes((W_M, tm), BF16))
    scratch = _nbytes((tm, D), BF16)
    last = nu - 1
    km_block = (3 * W_A + W_M) // TN_IN

    def w_row_block(j):
        return jnp.where(j < km_block, j, jnp.where(j < nu, j + 1, km_block))

    return pl.pallas_call(
        functools.partial(_inproj_kernel, n_u_tiles=nu),
        grid=(T // tm, nu + 1),
        in_specs=[
            pl.BlockSpec((tm, D), lambda i, j: (i, 0)),
            pl.BlockSpec((1, D), lambda i, j: (0, 0)),
            pl.BlockSpec((TN_IN, D), lambda i, j: (w_row_block(j), 0)),
            pl.BlockSpec((LANES, D), lambda i, j: (0, 0)),
            pl.BlockSpec((1, DH_A), lambda i, j: (0, 0)),
            pl.BlockSpec((1, DH_A), lambda i, j: (0, 0)),
        ],
        out_specs=[
            pl.BlockSpec((tm, TN_IN), lambda i, j: (i, jnp.minimum(j, last))),
            pl.BlockSpec((tm, LANES), lambda i, j: (i, 0)),
            pl.BlockSpec((W_M, tm), lambda i, j: (0, i)),
        ],
        out_shape=[
            jax.ShapeDtypeStruct((T, U_MAIN), BF16),
            jax.ShapeDtypeStruct((T, LANES), F32),
            jax.ShapeDtypeStruct((W_M, T), BF16),
        ],
        scratch_shapes=[pltpu.VMEM((tm, D), BF16)],
        compiler_params=pltpu.CompilerParams(
            dimension_semantics=("arbitrary", "arbitrary"),
            vmem_limit_bytes=_vmem_limit(blocks, scratch)),
        name="inproj",
    )(x, g.reshape(1, D), w_t, w_gate_t, g_qn.reshape(1, DH_A), g_kn.reshape(1, DH_A))


def _attn_bias_table(rpb):
    a = np.arange(ATT_RB)[:, None]
    ik = np.arange(ATT_WIN)[None, :]
    kh = WIN_H
    valid0 = ik < kh
    dr0 = ik - a + (WIN_H - 1)
    valid1 = (ik >= a) & (ik < a + kh)
    dr1 = ik - a + (WIN_H - 1) - kh // 2
    valid2 = (ik >= ATT_WIN - kh) & (ik < ATT_WIN)
    dr2 = (ik - ATT_WIN) - (a - ATT_RB) + (WIN_H - 1)
    valid = np.stack([np.broadcast_to(valid0, dr0.shape), valid1,
                      np.broadcast_to(valid2, dr0.shape)])
    dr = np.stack([dr0, dr1, dr2])

    c = np.arange(GRID_W)
    cs = np.clip(c - WIN_W // 2, 0, GRID_W - WIN_W)
    colmask = (c[None, :] >= cs[:, None]) & (c[None, :] < cs[:, None] + WIN_W)
    dc = np.clip(c[None, :] - c[:, None], -(WIN_W - 1), WIN_W - 1) + (WIN_W - 1)

    onehot = (dc[None] == np.arange(2 * WIN_W - 1)[:, None, None]).astype(np.float32)
    tiles = jnp.einsum("hrc,cqk->hqrk", rpb.astype(F32), jnp.asarray(onehot),
                       precision=lax.Precision.HIGHEST)
    tiles = jnp.where(jnp.asarray(colmask)[:, None, :], tiles, NEG_BIG)
    seq = tiles.reshape(H_A, GRID_W, (2 * WIN_H - 1) * GRID_W)
    classes = []
    for cls in range(3):
        rows = []
        for qa in range(ATT_RB):
            kk = np.nonzero(valid[cls, qa])[0]
            k0, k1 = int(kk[0]), int(kk[-1]) + 1
            d0 = int(dr[cls, qa, k0])
            assert np.array_equal(kk, np.arange(k0, k1))
            assert np.array_equal(dr[cls, qa, k0:k1], np.arange(d0, d0 + k1 - k0))
            strip = seq[:, :, d0 * GRID_W:(d0 + k1 - k0) * GRID_W]
            rows.append(jnp.pad(strip, ((0, 0), (0, 0), (k0 * GRID_W, (ATT_WIN - k1) * GRID_W)),
                                constant_values=NEG_BIG))
        classes.append(jnp.concatenate(rows, axis=1))
    return jnp.stack(classes)


def _attn_kernel(*refs, n_rows):
    q_ref = refs[0]
    k_refs = refs[1:1 + ATT_BLOCKS]
    v_refs = refs[1 + ATT_BLOCKS:1 + 2 * ATT_BLOCKS]
    bias_ref, o_ref = refs[1 + 2 * ATT_BLOCKS:]
    nb = n_rows // ATT_RB
    tq = ATT_RB * GRID_W

    def scores(unit):
        r, hh = divmod(unit, ATT_HEADS)
        i = pl.program_id(2) * ATT_BLOCKS + r
        cls = jnp.where(i == 0, 0, jnp.where(i == nb - 1, 2, 1))
        sl = slice(hh * DH_A, (hh + 1) * DH_A)
        q = q_ref[0, r * tq:(r + 1) * tq, sl]
        kw = k_refs[r][0, :, sl]
        s = lax.dot_general(q, kw, (((1,), (1,)), ((), ())), preferred_element_type=F32)
        s = s + bias_ref[cls, hh]
        return s, jnp.max(s, axis=-1, keepdims=True)

    def weighted_values(unit, s, m):
        r, hh = divmod(unit, ATT_HEADS)
        sl = slice(hh * DH_A, (hh + 1) * DH_A)
        p = jnp.exp(s - m)
        l = jnp.sum(p, axis=-1, keepdims=True)
        o = jnp.dot(p.astype(BF16), v_refs[r][0, :, sl], preferred_element_type=F32)
        o_ref[0, r * tq:(r + 1) * tq, sl] = (o / l).astype(BF16)

    pending = None
    for unit in range(ATT_BLOCKS * ATT_HEADS):
        cur = scores(unit)
        if pending is not None:
            weighted_values(*pending)
        pending = (unit, *cur)
    weighted_values(*pending)


def _attention(u, bias, B, S):
    R = S // GRID_W
    nblk = R // ATT_RB
    assert R % ATT_RB == 0 and R >= ATT_WIN and nblk >= 3 and nblk % ATT_BLOCKS == 0
    tq = ATT_RB * GRID_W
    tk = ATT_WIN * GRID_W
    hw = ATT_HEADS * DH_A
    ng = H_A // ATT_HEADS
    blocks = (_nbytes((ATT_BLOCKS * tq, hw), BF16) * 2 + ATT_BLOCKS * _nbytes((tk, hw), BF16) * 2
              + _nbytes((3, ATT_HEADS, tq, tk), F32))

    def win_start(i):
        return jnp.clip(i * ATT_RB - WIN_H // 2, 0, R - ATT_WIN) * GRID_W

    def window_spec(col_group, r):
        return pl.BlockSpec(
            (pl.Element(1), pl.Element(tk), pl.Element(hw)),
            lambda h, b, i: (b, win_start(i * ATT_BLOCKS + r), (col_group * ng + h) * hw))

    k_specs = [window_spec(1, r) for r in range(ATT_BLOCKS)]
    v_specs = [window_spec(2, r) for r in range(ATT_BLOCKS)]
    return pl.pallas_call(
        functools.partial(_attn_kernel, n_rows=R),
        grid=(ng, B, nblk // ATT_BLOCKS),
        in_specs=[
            pl.BlockSpec((1, ATT_BLOCKS * tq, hw), lambda h, b, i: (b, i, h)),
            *k_specs, *v_specs,
            pl.BlockSpec((3, ATT_HEADS, tq, tk), lambda h, b, i: (0, h, 0, 0)),
        ],
        out_specs=pl.BlockSpec((1, ATT_BLOCKS * tq, hw), lambda h, b, i: (b, i, h)),
        out_shape=jax.ShapeDtypeStruct((B, S, W_A), BF16),
        compiler_params=pltpu.CompilerParams(
            dimension_semantics=("arbitrary", "arbitrary", "arbitrary"),
            vmem_limit_bytes=_vmem_limit(blocks, 0)),
        name="attention",
    )(u, *([u] * (2 * ATT_BLOCKS)), bias)


def _mlstm_kernel(*refs, reverse, n_chunks):
    bi_ref, bf_ref, q_ref = refs[:3]
    kt_refs = refs[3:3 + ML_BATCH]
    rest = refs[3 + ML_BATCH:]
    if reverse:
        (v_ref, gi_ref, gf_ref, hf_ref, og_ref, gmh_ref,
         o_ref, ct_ref, m_ref, b_ref, e_ref) = rest
    else:
        (v_ref, gi_ref, gf_ref, o_ref, ct_ref, m_ref, b_ref, e_ref) = rest
    L = ML_CHUNK
    c = pl.program_id(1)
    d = 1 if reverse else 0

    row = lax.broadcasted_iota(jnp.int32, (L, L), 0)
    col = lax.broadcasted_iota(jnp.int32, (L, L), 1)
    tri = (col >= row) if reverse else (col <= row)
    eye = row == col

    @pl.when(c == 0)
    def _():
        ct_ref[...] = jnp.zeros_like(ct_ref)
        m_ref[...] = jnp.zeros_like(m_ref)
        cum = jnp.where((row >= col) if reverse else (row <= col), 1.0, 0.0).astype(F32)
        for bb in range(ML_BATCH):
            for hh in range(H_M):
                ig = gi_ref[bb, hh] + bi_ref[d, hh]
                lf = jax.nn.log_sigmoid(gf_ref[bb, hh] + bf_ref[d, hh])
                b = jnp.dot(lf, cum, preferred_element_type=F32, precision=lax.Precision.HIGHEST)
                b_ref[bb * H_M + hh] = b
                e_ref[bb * H_M + hh] = ig - b

    cc = (n_chunks - 1 - c) if reverse else c

    def chain(ch):
        bb, hh = divmod(ch, H_M)
        sl = slice(hh * DH_M, (hh + 1) * DH_M)
        b_row = b_ref[ch, pl.ds(cc, 1), :]
        e_row = e_ref[ch, pl.ds(cc, 1), :]
        b_col = jnp.sum(jnp.where(eye, b_row, 0.0), axis=1, keepdims=True)
        e_col = jnp.sum(jnp.where(eye, e_row, 0.0), axis=1, keepdims=True)
        a_tot = b_row[:, 0:1] if reverse else b_row[:, L - 1:L]
        m_prev = m_ref[ch]
        yield

        q = q_ref[bb, :, sl]
        kt = kt_refs[bb][sl, :]
        v = v_ref[bb, :, sl]

        dmat = jnp.where(tri, b_col + e_row, -jnp.inf)
        inter = b_col + m_prev
        m_t = jnp.maximum(inter, jnp.max(dmat, axis=1, keepdims=True))
        s = jnp.dot(q, kt, preferred_element_type=F32)
        ct = ct_ref[ch]
        inter_aug = jnp.dot(q, ct.astype(BF16), preferred_element_type=F32)
        yield

        dexp = jnp.exp(dmat - m_t)
        sqk = s * dexp
        sc = jnp.exp(inter - m_t)
        yield

        num = sc * inter_aug[:, :DH_M] + jnp.dot(sqk.astype(BF16), v, preferred_element_type=F32)
        den = sc * inter_aug[:, DH_M:] + jnp.sum(sqk, axis=1, keepdims=True)
        rden = 1.0 / jnp.maximum(jnp.abs(den), jnp.exp(-m_t))
        hdir = num * jnp.concatenate([rden] * (DH_M // LANES), axis=1)

        if reverse:
            hm = _rms_rows(hf_ref[bb, :, sl] + hdir, gmh_ref[:, sl])
            o_ref[bb, :, sl] = (jax.nn.sigmoid(og_ref[bb, :, sl].astype(F32)) * hm).astype(BF16)
        else:
            o_ref[bb, :, sl] = hdir
        yield

        g_col = a_tot + e_col
        m_loc = jnp.max(g_col, axis=0, keepdims=True)
        m_new = jnp.maximum(a_tot + m_prev, m_loc)
        s_prev = jnp.exp(a_tot + m_prev - m_new)
        w_col = jnp.exp(g_col - m_new)
        vw_aug = jnp.concatenate(
            [(v.astype(F32) * w_col).astype(BF16),
             jnp.broadcast_to(w_col, (L, LANES)).astype(BF16)], axis=1)
        ct_ref[ch] = s_prev * ct + jnp.dot(kt, vw_aug, preferred_element_type=F32)
        m_ref[ch] = m_new

    chains = [chain(ch) for ch in range(ML_BATCH * H_M)]
    while chains:
        chains = [g for g in chains if next(g, StopIteration) is not StopIteration]


def _mlstm_sweep(u, kt, gates_t, b_igate, b_fgate, B, S, *, reverse, hf=None, g_mh=None):
    L = ML_CHUNK
    N = S // L
    d = 1 if reverse else 0
    qoff = 3 * W_A // W_M
    cidx = (lambda c: N - 1 - c) if reverse else (lambda c: c)
    smem = pl.BlockSpec(memory_space=pltpu.SMEM)
    nb = ML_BATCH
    assert B % nb == 0
    nch = nb * H_M
    kt_specs = [pl.BlockSpec((W_M, L), functools.partial(
        lambda b, c, r: (0, (nb * b + r) * N + cidx(c)), r=r)) for r in range(nb)]
    in_specs = [
        smem, smem,
        pl.BlockSpec((nb, L, W_M), lambda b, c: (b, cidx(c), qoff)),
        *kt_specs,
        pl.BlockSpec((nb, L, W_M), lambda b, c: (b, cidx(c), qoff + 1)),
        pl.BlockSpec((nb, H_M, N, L), lambda b, c: (b, d, 0, 0)),
        pl.BlockSpec((nb, H_M, N, L), lambda b, c: (b, 2 + d, 0, 0)),
    ]
    args = [b_igate, b_fgate, u, *([kt] * nb), u, gates_t, gates_t]
    if reverse:
        in_specs += [
            pl.BlockSpec((nb, L, W_M), lambda b, c: (b, cidx(c), 0)),
            pl.BlockSpec((nb, L, W_M), lambda b, c: (b, cidx(c), qoff + 2)),
            pl.BlockSpec((1, W_M), lambda b, c: (0, 0)),
        ]
        args += [hf, u, g_mh.reshape(1, W_M)]
        out_dtype = BF16
    else:
        out_dtype = F32
    blocks = nb * (_nbytes((L, W_M), BF16) * 4 + _nbytes((H_M, N, L), F32) * 2
                   + _nbytes((L, W_M), F32) * 2)
    scratch = (nch * _nbytes((DH_M, DH_M + LANES), F32) + 2 * nch * _nbytes((N, L), F32)
               + nch * 8 * LANES * 4)
    return pl.pallas_call(
        functools.partial(_mlstm_kernel, reverse=reverse, n_chunks=N),
        grid=(B // nb, N),
        in_specs=in_specs,
        out_specs=pl.BlockSpec((nb, L, W_M), lambda b, c: (b, cidx(c), 0)),
        out_shape=jax.ShapeDtypeStruct((B, S, W_M), out_dtype),
        scratch_shapes=[
            pltpu.VMEM((nch, DH_M, DH_M + LANES), F32),
            pltpu.VMEM((nch, 1, 1), F32),
            pltpu.VMEM((nch, N, L), F32),
            pltpu.VMEM((nch, N, L), F32),
        ],
        compiler_params=pltpu.CompilerParams(
            dimension_semantics=("arbitrary", "arbitrary"),
            vmem_limit_bytes=_vmem_limit(blocks, scratch)),
        name="mlstm_bwd" if reverse else "mlstm_fwd",
    )(*args)


def _outproj_kernel(x_ref, ya_ref, ym_ref, wa_ref, wm_ref, o_ref):
    o_ref[...] = (x_ref[...]
                  + jnp.dot(ya_ref[...], wa_ref[...], preferred_element_type=F32)
                  + jnp.dot(ym_ref[...], wm_ref[...], preferred_element_type=F32))


def _outproj(x, ya, ym, w_a, w_m):
    T, D = x.shape
    blocks = (_nbytes((TM, D), F32) * 2 + _nbytes((TM, W_A), BF16) * 2 + _nbytes((W_A, D), BF16) * 2)
    return pl.pallas_call(
        _outproj_kernel,
        grid=(T // TM,),
        in_specs=[
            pl.BlockSpec((TM, D), lambda i: (i, 0)),
            pl.BlockSpec((TM, W_A), lambda i: (i, 0)),
            pl.BlockSpec((TM, W_M), lambda i: (i, 0)),
            pl.BlockSpec((W_A, D), lambda i: (0, 0)),
            pl.BlockSpec((W_M, D), lambda i: (0, 0)),
        ],
        out_specs=pl.BlockSpec((TM, D), lambda i: (i, 0)),
        out_shape=jax.ShapeDtypeStruct((T, D), F32),
        compiler_params=pltpu.CompilerParams(
            dimension_semantics=("arbitrary",),
            vmem_limit_bytes=_vmem_limit(blocks, 0)),
        name="outproj",
    )(x, ya, ym, w_a, w_m)


def _ple_kernel(x_ref, g_ref, pe_ref, wg_ref, wp_ref, o_ref):
    x = x_ref[...]
    hn = _rms_rows(x, g_ref[...]).astype(BF16)
    gate = jax.nn.sigmoid(jnp.dot(hn, wg_ref[...], preferred_element_type=F32))
    proj = jnp.dot(pe_ref[...].astype(BF16), wp_ref[...], preferred_element_type=F32)
    o_ref[...] = x + gate * proj


def _ple(x, g, pe, w_gate, w_proj):
    T, D = x.shape
    P = pe.shape[1]
    blocks = (_nbytes((TM, D), F32) * 2 + _nbytes((TM, P), F32) + _nbytes((D, D), BF16)
              + _nbytes((P, D), BF16))
    return pl.pallas_call(
        _ple_kernel,
        grid=(T // TM,),
        in_specs=[
            pl.BlockSpec((TM, D), lambda i: (i, 0)),
            pl.BlockSpec((1, D), lambda i: (0, 0)),
            pl.BlockSpec((TM, P), lambda i: (i, 0)),
            pl.BlockSpec((D, D), lambda i: (0, 0)),
            pl.BlockSpec((P, D), lambda i: (0, 0)),
        ],
        out_specs=pl.BlockSpec((TM, D), lambda i: (i, 0)),
        out_shape=jax.ShapeDtypeStruct((T, D), F32),
        compiler_params=pltpu.CompilerParams(
            dimension_semantics=("arbitrary",),
            vmem_limit_bytes=_vmem_limit(blocks, 0)),
        name="ple",
    )(x, g.reshape(1, D), pe, w_gate, w_proj)


def _layer(x, pe, w):
    B, S, D = x.shape
    T = B * S
    x2 = x.reshape(T, D)
    x2 = _ffn(x2, w["g_ffn1"], w["w1g"], w["w1u"], w["w1d"])

    u, gates, kt = _inproj(x2, w["g_mix"], w["w_in_t"], w["w_in_gate_t"], w["g_qn"], w["g_kn"])
    u = u.reshape(B, S, U_MAIN)
    ya = _attention(u, w["attn_bias"], B, S)

    n_chunks = S // ML_CHUNK
    gates_t = jnp.transpose(gates[:, :N_GATES].reshape(B, S, N_GATES), (0, 2, 1))
    gates_t = gates_t.reshape(B, N_GATES, n_chunks, ML_CHUNK)
    hf = _mlstm_sweep(u, kt, gates_t, w["b_igate"], w["b_fgate"], B, S, reverse=False)
    ym = _mlstm_sweep(u, kt, gates_t, w["b_igate"], w["b_fgate"], B, S, reverse=True,
                      hf=hf, g_mh=w["g_mh"])

    x2 = _outproj(x2, ya.reshape(T, W_A), ym.reshape(T, W_M), w["w_out_a"], w["w_out_m"])
    x2 = _ffn(x2, w["g_ffn2"], w["w2g"], w["w2u"], w["w2d"])
    x2 = _ple(x2, w["g_ple"], pe.reshape(T, -1), w["w_ple_gate"], w["w_ple_proj"])
    return x2.reshape(B, S, D)


def kernel(x_prompt, x_sample, p_prompt, p_sample, g_ffn1, w_ffn1_gate, w_ffn1_up, w_ffn1_down, g_mix, w_in, b_igate, b_fgate, g_qn, g_kn, rpb, g_mh, w_out, g_ffn2, w_ffn2_gate, w_ffn2_up, w_ffn2_down, g_ple, w_ple_gate, w_ple_proj):
    depth = g_ffn1.shape[0]
    xs = [x_prompt, x_sample]
    ps = [p_prompt, p_sample]
    for i in range(depth):
        w_in_t = jnp.transpose(w_in[i]).astype(BF16)
        u_end = 3 * W_A + 4 * W_M
        gate_rows = jnp.pad(w_in_t[u_end:], ((0, LANES - N_GATES), (0, 0)))
        w = {
            "g_ffn1": g_ffn1[i],
            "w1g": w_ffn1_gate[i].astype(BF16), "w1u": w_ffn1_up[i].astype(BF16),
            "w1d": w_ffn1_down[i],
            "g_mix": g_mix[i],
            "w_in_t": w_in_t,
            "w_in_gate_t": gate_rows,
            "b_igate": b_igate[i], "b_fgate": b_fgate[i], "g_qn": g_qn[i], "g_kn": g_kn[i],
            "attn_bias": _attn_bias_table(rpb[i]), "g_mh": g_mh[i],
            "w_out_a": w_out[i, :W_A].astype(BF16), "w_out_m": w_out[i, W_A:].astype(BF16),
            "g_ffn2": g_ffn2[i],
            "w2g": w_ffn2_gate[i].astype(BF16), "w2u": w_ffn2_up[i].astype(BF16),
            "w2d": w_ffn2_down[i],
            "g_ple": g_ple[i], "w_ple_gate": w_ple_gate[i].astype(BF16),
            "w_ple_proj": w_ple_proj[i].astype(BF16),
        }
        xs = [_layer(x, p[i], w) for x, p in zip(xs, ps)]
    return (xs[0], xs[1])
```

```python
import functools

import numpy as np
import jax
import jax.numpy as jnp
from jax import lax
from jax.experimental import pallas as pl
from jax.experimental.pallas import tpu as pltpu

F32 = jnp.float32
BF16 = jnp.bfloat16

GRID_W = 64
WIN_H = 8
WIN_W = 16
H_A = 8
DH_A = 128
H_M = 4
DH_M = 256
W_A = H_A * DH_A
W_M = H_M * DH_M
EPS = 1e-6
N_GATES = 4 * H_M
U_MAIN = 3 * W_A + 3 * W_M

LANES = 128
V7X_VMEM_BYTES = 64 * 1024 * 1024

TM = 512
TM_FFN = 1024
TM_IN = 1024
TF = 512
TN_IN = 1024
ATT_RB = 4
ATT_WIN = 12
ATT_HEADS = 8
ML_CHUNK = 256
ML_BATCH = 2
NEG_BIG = -1e30


def _vmem_limit(block_bytes, scratch_bytes):
    est = 2 * block_bytes + scratch_bytes + 12 * 1024 * 1024
    return int(min(est, V7X_VMEM_BYTES - 6 * 1024 * 1024))


def _nbytes(shape, dtype):
    return int(np.prod(shape)) * jnp.dtype(dtype).itemsize


def _rms_rows(x, g):
    ms = jnp.mean(x * x, axis=-1, keepdims=True)
    return x * lax.rsqrt(ms + EPS) * g


def _ffn_kernel(x_ref, g_ref, wg_ref, wu_ref, wd_ref, o_ref, hn_ref):
    j = pl.program_id(1)

    def half_step(hn):
        a = jnp.dot(hn, wg_ref[...], preferred_element_type=F32)
        b = jnp.dot(hn, wu_ref[...], preferred_element_type=F32)
        h = (a * jax.nn.sigmoid(a) * (0.5 * b)).astype(BF16)
        return jnp.dot(h, wd_ref[...].astype(BF16), preferred_element_type=F32)

    @pl.when(j == 0)
    def _():
        x = x_ref[...]
        hn = _rms_rows(x, g_ref[...]).astype(BF16)
        hn_ref[...] = hn
        o_ref[...] = x + half_step(hn)

    @pl.when(j > 0)
    def _():
        o_ref[...] += half_step(hn_ref[...])


def _ffn(x, g, wg, wu, wd):
    T, D = x.shape
    F = wd.shape[0]
    tm = TM_FFN
    blocks = (_nbytes((tm, D), F32) * 2 + _nbytes((D, 2 * TF), BF16) + _nbytes((TF, D), F32))
    scratch = _nbytes((tm, D), BF16)
    return pl.pallas_call(
        _ffn_kernel,
        grid=(T // tm, F // TF),
        in_specs=[
            pl.BlockSpec((tm, D), lambda i, j: (i, 0)),
            pl.BlockSpec((1, D), lambda i, j: (0, 0)),
            pl.BlockSpec((D, TF), lambda i, j: (0, j)),
            pl.BlockSpec((D, TF), lambda i, j: (0, j)),
            pl.BlockSpec((TF, D), lambda i, j: (j, 0)),
        ],
        out_specs=pl.BlockSpec((tm, D), lambda i, j: (i, 0)),
        out_shape=jax.ShapeDtypeStruct((T, D), F32),
        scratch_shapes=[pltpu.VMEM((tm, D), BF16)],
        compiler_params=pltpu.CompilerParams(
            dimension_semantics=("arbitrary", "arbitrary"),
            vmem_limit_bytes=_vmem_limit(blocks, scratch)),
        name="ffn",
    )(x, g.reshape(1, D), wg, wu, wd)


_NT = (((1,), (1,)), ((), ()))


def _inproj_kernel(x_ref, g_ref, wt_ref, wgate_ref, gq_ref, gk_ref,
                   u_ref, gates_ref, kt_ref, hn_ref, *, n_u_tiles):
    j = pl.program_id(1)

    n_q_tiles = W_A // TN_IN
    n_qk_tiles = 2 * n_q_tiles

    def project(hn):
        return lax.dot_general(hn, wt_ref[...], _NT, preferred_element_type=F32)

    def store_qk_normed(acc, gain):
        for hh in range(TN_IN // DH_A):
            sl = slice(hh * DH_A, (hh + 1) * DH_A)
            u_ref[:, sl] = _rms_rows(acc[:, sl], gain).astype(BF16)

    @pl.when(j == 0)
    def _():
        hn = _rms_rows(x_ref[...], g_ref[...]).astype(BF16)
        hn_ref[...] = hn
        gates_ref[...] = lax.dot_general(hn, wgate_ref[...], _NT, preferred_element_type=F32)
        store_qk_normed(project(hn), gq_ref[...] * (DH_A ** -0.5))

    @pl.when((j > 0) & (j < n_qk_tiles))
    def _():
        gain = jnp.where(j < n_q_tiles, gq_ref[...] * (DH_A ** -0.5), gk_ref[...])
        store_qk_normed(project(hn_ref[...]), gain)

    @pl.when((j >= n_qk_tiles) & (j < n_u_tiles))
    def _():
        u_ref[...] = project(hn_ref[...]).astype(BF16)

    @pl.when(j == n_u_tiles)
    def _():
        kt = lax.dot_general(wt_ref[...], hn_ref[...], _NT, preferred_element_type=F32)
        kt_ref[...] = (kt * (DH_M ** -0.5)).astype(BF16)


def _inproj(x, g, w_t, w_gate_t, g_qn, g_kn):
    T, D = x.shape
    tm = TM_IN
    nu = U_MAIN // TN_IN
    assert W_M == TN_IN
    blocks = (_nbytes((tm, D), F32) + _nbytes((TN_IN, D), BF16) + _nbytes((LANES, D), BF16)
              + _nbytes((tm, TN_IN), BF16) + _nbytes((tm, LANES), F32) + _nbytes((W_M, tm), BF16))
    scratch = _nbytes((tm, D), BF16)
    last = nu - 1
    km_block = (3 * W_A + W_M) // TN_IN

    def w_row_block(j):
        return jnp.where(j < km_block, j, jnp.where(j < nu, j + 1, km_block))

    return pl.pallas_call(
        functools.partial(_inproj_kernel, n_u_tiles=nu),
        grid=(T // tm, nu + 1),
        in_specs=[
            pl.BlockSpec((tm, D), lambda i, j: (i, 0)),
            pl.BlockSpec((1, D), lambda i, j: (0, 0)),
            pl.BlockSpec((TN_IN, D), lambda i, j: (w_row_block(j), 0)),
            pl.BlockSpec((LANES, D), lambda i, j: (0, 0)),
            pl.BlockSpec((1, DH_A), lambda i, j: (0, 0)),
            pl.BlockSpec((1, DH_A), lambda i, j: (0, 0)),
        ],
        out_specs=[
            pl.BlockSpec((tm, TN_IN), lambda i, j: (i, jnp.minimum(j, last))),
            pl.BlockSpec((tm, LANES), lambda i, j: (i, 0)),
            pl.BlockSpec((W_M, tm), lambda i, j: (0, i)),
        ],
        out_shape=[
            jax.ShapeDtypeStruct((T, U_MAIN), BF16),
            jax.ShapeDtypeStruct((T, LANES), F32),
            jax.ShapeDtypeStruct((W_M, T), BF16),
        ],
        scratch_shapes=[pltpu.VMEM((tm, D), BF16)],
        compiler_params=pltpu.CompilerParams(
            dimension_semantics=("arbitrary", "arbitrary"),
            vmem_limit_bytes=_vmem_limit(blocks, scratch)),
        name="inproj",
    )(x, g.reshape(1, D), w_t, w_gate_t, g_qn.reshape(1, DH_A), g_kn.reshape(1, DH_A))


def _attn_bias_table(rpb):
    a = np.arange(ATT_RB)[:, None]
    ik = np.arange(ATT_WIN)[None, :]
    kh = WIN_H
    valid0 = ik < kh
    dr0 = ik - a + (WIN_H - 1)
    valid1 = (ik >= a) & (ik < a + kh)
    dr1 = ik - a + (WIN_H - 1) - kh // 2
    valid2 = (ik >= ATT_WIN - kh) & (ik < ATT_WIN)
    dr2 = (ik - ATT_WIN) - (a - ATT_RB) + (WIN_H - 1)
    valid = np.stack([np.broadcast_to(valid0, dr0.shape), valid1,
                      np.broadcast_to(valid2, dr0.shape)])
    dr = np.stack([dr0, dr1, dr2])

    c = np.arange(GRID_W)
    cs = np.clip(c - WIN_W // 2, 0, GRID_W - WIN_W)
    colmask = (c[None, :] >= cs[:, None]) & (c[None, :] < cs[:, None] + WIN_W)
    dc = np.clip(c[None, :] - c[:, None], -(WIN_W - 1), WIN_W - 1) + (WIN_W - 1)

    onehot = (dc[None] == np.arange(2 * WIN_W - 1)[:, None, None]).astype(np.float32)
    tiles = jnp.einsum("hrc,cqk->hqrk", rpb.astype(F32), jnp.asarray(onehot),
                       precision=lax.Precision.HIGHEST)
    tiles = jnp.where(jnp.asarray(colmask)[:, None, :], tiles, NEG_BIG)
    seq = tiles.reshape(H_A, GRID_W, (2 * WIN_H - 1) * GRID_W)
    classes = []
    for cls in range(3):
        rows = []
        for qa in range(ATT_RB):
            kk = np.nonzero(valid[cls, qa])[0]
            k0, k1 = int(kk[0]), int(kk[-1]) + 1
            d0 = int(dr[cls, qa, k0])
            assert np.array_equal(kk, np.arange(k0, k1))
            assert np.array_equal(dr[cls, qa, k0:k1], np.arange(d0, d0 + k1 - k0))
            strip = seq[:, :, d0 * GRID_W:(d0 + k1 - k0) * GRID_W]
            rows.append(jnp.pad(strip, ((0, 0), (0, 0), (k0 * GRID_W, (ATT_WIN - k1) * GRID_W)),
                                constant_values=NEG_BIG))
        classes.append(jnp.concatenate(rows, axis=1))
    return jnp.stack(classes)


def _attn_kernel(q_ref, k_ref, v_ref, bias_ref, o_ref, *, n_rows):
    i = pl.program_id(2)
    nb = n_rows // ATT_RB
    cls = jnp.where(i == 0, 0, jnp.where(i == nb - 1, 2, 1))

    def scores(hh):
        sl = slice(hh * DH_A, (hh + 1) * DH_A)
        q = q_ref[0, :, sl]
        kw = k_ref[0, :, sl]
        s = lax.dot_general(q, kw, (((1,), (1,)), ((), ())), preferred_element_type=F32)
        s = s + bias_ref[cls, hh]
        return s, jnp.max(s, axis=-1, keepdims=True)

    def weighted_values(hh, s, m):
        sl = slice(hh * DH_A, (hh + 1) * DH_A)
        p = jnp.exp(s - m)
        l = jnp.sum(p, axis=-1, keepdims=True)
        o = jnp.dot(p.astype(BF16), v_ref[0, :, sl], preferred_element_type=F32)
        o_ref[0, :, sl] = (o / l).astype(BF16)

    pending = None
    for hh in range(ATT_HEADS):
        cur = scores(hh)
        if pending is not None:
            weighted_values(*pending)
        pending = (hh, *cur)
    weighted_values(*pending)


def _attention(u, bias, B, S):
    R = S // GRID_W
    assert R % ATT_RB == 0 and R >= ATT_WIN and R // ATT_RB >= 3
    tq = ATT_RB * GRID_W
    tk = ATT_WIN * GRID_W
    hw = ATT_HEADS * DH_A
    ng = H_A // ATT_HEADS
    blocks = (_nbytes((tq, hw), BF16) * 2 + _nbytes((tk, hw), BF16) * 2
              + _nbytes((3, ATT_HEADS, tq, tk), F32))

    def win_start(i):
        return jnp.clip(i * ATT_RB - WIN_H // 2, 0, R - ATT_WIN) * GRID_W

    return pl.pallas_call(
        functools.partial(_attn_kernel, n_rows=R),
        grid=(ng, B, R // ATT_RB),
        in_specs=[
            pl.BlockSpec((1, tq, hw), lambda h, b, i: (b, i, h)),
            pl.BlockSpec((pl.Element(1), pl.Element(tk), pl.Element(hw)),
                         lambda h, b, i: (b, win_start(i), (ng + h) * hw)),
            pl.BlockSpec((pl.Element(1), pl.Element(tk), pl.Element(hw)),
                         lambda h, b, i: (b, win_start(i), (2 * ng + h) * hw)),
            pl.BlockSpec((3, ATT_HEADS, tq, tk), lambda h, b, i: (0, h, 0, 0)),
        ],
        out_specs=pl.BlockSpec((1, tq, hw), lambda h, b, i: (b, i, h)),
        out_shape=jax.ShapeDtypeStruct((B, S, W_A), BF16),
        compiler_params=pltpu.CompilerParams(
            dimension_semantics=("arbitrary", "arbitrary", "arbitrary"),
            vmem_limit_bytes=_vmem_limit(blocks, 0)),
        name="attention",
    )(u, u, u, bias)


def _mlstm_kernel(*refs, reverse, n_chunks):
    bi_ref, bf_ref, q_ref = refs[:3]
    kt_refs = refs[3:3 + ML_BATCH]
    rest = refs[3 + ML_BATCH:]
    if reverse:
        (v_ref, gi_ref, gf_ref, hf_ref, o_ref, ct_ref, m_ref, b_ref, e_ref) = rest
    else:
        (v_ref, gi_ref, gf_ref, o_ref, ct_ref, m_ref, b_ref, e_ref) = rest
    L = ML_CHUNK
    c = pl.program_id(1)
    d = 1 if reverse else 0

    row = lax.broadcasted_iota(jnp.int32, (L, L), 0)
    col = lax.broadcasted_iota(jnp.int32, (L, L), 1)
    tri = (col >= row) if reverse else (col <= row)
    eye = row == col

    @pl.when(c == 0)
    def _():
        ct_ref[...] = jnp.zeros_like(ct_ref)
        m_ref[...] = jnp.zeros_like(m_ref)
        cum = jnp.where((row >= col) if reverse else (row <= col), 1.0, 0.0).astype(F32)
        for bb in range(ML_BATCH):
            for hh in range(H_M):
                ig = gi_ref[bb, hh] + bi_ref[d, hh]
                lf = jax.nn.log_sigmoid(gf_ref[bb, hh] + bf_ref[d, hh])
                b = jnp.dot(lf, cum, preferred_element_type=F32, precision=lax.Precision.HIGHEST)
                b_ref[bb * H_M + hh] = b
                e_ref[bb * H_M + hh] = ig - b

    cc = (n_chunks - 1 - c) if reverse else c

    def chain(ch):
        bb, hh = divmod(ch, H_M)
        sl = slice(hh * DH_M, (hh + 1) * DH_M)
        b_row = b_ref[ch, pl.ds(cc, 1), :]
        e_row = e_ref[ch, pl.ds(cc, 1), :]
        b_col = jnp.sum(jnp.where(eye, b_row, 0.0), axis=1, keepdims=True)
        e_col = jnp.sum(jnp.where(eye, e_row, 0.0), axis=1, keepdims=True)
        a_tot = b_row[:, 0:1] if reverse else b_row[:, L - 1:L]
        m_prev = m_ref[ch]
        yield

        q = q_ref[bb, :, sl]
        kt = kt_refs[bb][sl, :]
        v = v_ref[bb, :, sl]

        dmat = jnp.where(tri, b_col + e_row, -jnp.inf)
        inter = b_col + m_prev
        m_t = jnp.maximum(inter, jnp.max(dmat, axis=1, keepdims=True))
        s = jnp.dot(q, kt, preferred_element_type=F32)
        ct = ct_ref[ch]
        inter_aug = jnp.dot(q, ct.astype(BF16), preferred_element_type=F32)
        yield

        dexp = jnp.exp(dmat - m_t)
        sqk = s * dexp
        sc = jnp.exp(inter - m_t)
        yield

        num = sc * inter_aug[:, :DH_M] + jnp.dot(sqk.astype(BF16), v, preferred_element_type=F32)
        den = sc * inter_aug[:, DH_M:] + jnp.sum(sqk, axis=1, keepdims=True)
        rden = 1.0 / jnp.maximum(jnp.abs(den), jnp.exp(-m_t))
        hdir = num * jnp.concatenate([rden] * (DH_M // LANES), axis=1)

        if reverse:
            o_ref[bb, :, sl] = hf_ref[bb, :, sl] + hdir
        else:
            o_ref[bb, :, sl] = hdir
        yield

        g_col = a_tot + e_col
        m_loc = jnp.max(g_col, axis=0, keepdims=True)
        m_new = jnp.maximum(a_tot + m_prev, m_loc)
        s_prev = jnp.exp(a_tot + m_prev - m_new)
        w_col = jnp.exp(g_col - m_new)
        vw_aug = jnp.concatenate(
            [(v.astype(F32) * w_col).astype(BF16),
             jnp.broadcast_to(w_col, (L, LANES)).astype(BF16)], axis=1)
        ct_ref[ch] = s_prev * ct + jnp.dot(kt, vw_aug, preferred_element_type=F32)
        m_ref[ch] = m_new

    chains = [chain(ch) for ch in range(ML_BATCH * H_M)]
    while chains:
        chains = [g for g in chains if next(g, StopIteration) is not StopIteration]


def _mlstm_sweep(u, kt, gates_t, b_igate, b_fgate, B, S, *, reverse, hf=None):
    L = ML_CHUNK
    N = S // L
    d = 1 if reverse else 0
    qoff = 3 * W_A // W_M
    cidx = (lambda c: N - 1 - c) if reverse else (lambda c: c)
    smem = pl.BlockSpec(memory_space=pltpu.SMEM)
    nb = ML_BATCH
    assert B % nb == 0
    nch = nb * H_M
    kt_specs = [pl.BlockSpec((W_M, L), functools.partial(
        lambda b, c, r: (0, (nb * b + r) * N + cidx(c)), r=r)) for r in range(nb)]
    in_specs = [
        smem, smem,
        pl.BlockSpec((nb, L, W_M), lambda b, c: (b, cidx(c), qoff)),
        *kt_specs,
        pl.BlockSpec((nb, L, W_M), lambda b, c: (b, cidx(c), qoff + 1)),
        pl.BlockSpec((nb, H_M, N, L), lambda b, c: (b, d, 0, 0)),
        pl.BlockSpec((nb, H_M, N, L), lambda b, c: (b, 2 + d, 0, 0)),
    ]
    args = [b_igate, b_fgate, u, *([kt] * nb), u, gates_t, gates_t]
    if reverse:
        in_specs += [pl.BlockSpec((nb, L, W_M), lambda b, c: (b, cidx(c), 0))]
        args += [hf]
    out_dtype = F32
    blocks = nb * (_nbytes((L, W_M), BF16) * 4 + _nbytes((H_M, N, L), F32) * 2
                   + _nbytes((L, W_M), F32) * 2)
    scratch = (nch * _nbytes((DH_M, DH_M + LANES), F32) + 2 * nch * _nbytes((N, L), F32)
               + nch * 8 * LANES * 4)
    return pl.pallas_call(
        functools.partial(_mlstm_kernel, reverse=reverse, n_chunks=N),
        grid=(B // nb, N),
        in_specs=in_specs,
        out_specs=pl.BlockSpec((nb, L, W_M), lambda b, c: (b, cidx(c), 0)),
        out_shape=jax.ShapeDtypeStruct((B, S, W_M), out_dtype),
        scratch_shapes=[
            pltpu.VMEM((nch, DH_M, DH_M + LANES), F32),
            pltpu.VMEM((nch, 1, 1), F32),
            pltpu.VMEM((nch, N, L), F32),
            pltpu.VMEM((nch, N, L), F32),
        ],
        compiler_params=pltpu.CompilerParams(
            dimension_semantics=("arbitrary", "arbitrary"),
            vmem_limit_bytes=_vmem_limit(blocks, scratch)),
        name="mlstm_bwd" if reverse else "mlstm_fwd",
    )(*args)


def _outproj_kernel(x_ref, ya_ref, hm_ref, og_ref, gmh_ref, wa_ref, wm_ref, o_ref):
    ym = jnp.concatenate(
        [(jax.nn.sigmoid(og_ref[:, sl].astype(F32)) * _rms_rows(hm_ref[:, sl], gmh_ref[:, sl])
          ).astype(BF16)
         for sl in (slice(h * DH_M, (h + 1) * DH_M) for h in range(H_M))], axis=1)
    o_ref[...] = (x_ref[...]
                  + jnp.dot(ya_ref[...], wa_ref[...], preferred_element_type=F32)
                  + jnp.dot(ym, wm_ref[...], preferred_element_type=F32))


def _outproj(x, ya, hm, u, g_mh, w_a, w_m):
    T, D = x.shape
    og_block = (3 * W_A + 2 * W_M) // W_M
    blocks = (_nbytes((TM, D), F32) * 2 + _nbytes((TM, W_A), BF16) * 2 + _nbytes((TM, W_M), F32)
              + _nbytes((W_A, D), BF16) * 2)
    return pl.pallas_call(
        _outproj_kernel,
        grid=(T // TM,),
        in_specs=[
            pl.BlockSpec((TM, D), lambda i: (i, 0)),
            pl.BlockSpec((TM, W_A), lambda i: (i, 0)),
            pl.BlockSpec((TM, W_M), lambda i: (i, 0)),
            pl.BlockSpec((TM, W_M), lambda i: (i, og_block)),
            pl.BlockSpec((1, W_M), lambda i: (0, 0)),
            pl.BlockSpec((W_A, D), lambda i: (0, 0)),
            pl.BlockSpec((W_M, D), lambda i: (0, 0)),
        ],
        out_specs=pl.BlockSpec((TM, D), lambda i: (i, 0)),
        out_shape=jax.ShapeDtypeStruct((T, D), F32),
        compiler_params=pltpu.CompilerParams(
            dimension_semantics=("arbitrary",),
            vmem_limit_bytes=_vmem_limit(blocks, 0)),
        name="outproj",
    )(x, ya, hm, u, g_mh.reshape(1, W_M), w_a, w_m)


def _ple_kernel(x_ref, g_ref, pe_ref, wg_ref, wp_ref, o_ref):
    x = x_ref[...]
    hn = _rms_rows(x, g_ref[...]).astype(BF16)
    gate = jax.nn.sigmoid(jnp.dot(hn, wg_ref[...], preferred_element_type=F32))
    proj = jnp.dot(pe_ref[...].astype(BF16), wp_ref[...], preferred_element_type=F32)
    o_ref[...] = x + gate * proj


def _ple(x, g, pe, w_gate, w_proj):
    T, D = x.shape
    P = pe.shape[1]
    blocks = (_nbytes((TM, D), F32) * 2 + _nbytes((TM, P), F32) + _nbytes((D, D), BF16)
              + _nbytes((P, D), BF16))
    return pl.pallas_call(
        _ple_kernel,
        grid=(T // TM,),
        in_specs=[
            pl.BlockSpec((TM, D), lambda i: (i, 0)),
            pl.BlockSpec((1, D), lambda i: (0, 0)),
            pl.BlockSpec((TM, P), lambda i: (i, 0)),
            pl.BlockSpec((D, D), lambda i: (0, 0)),
            pl.BlockSpec((P, D), lambda i: (0, 0)),
        ],
        out_specs=pl.BlockSpec((TM, D), lambda i: (i, 0)),
        out_shape=jax.ShapeDtypeStruct((T, D), F32),
        compiler_params=pltpu.CompilerParams(
            dimension_semantics=("arbitrary",),
            vmem_limit_bytes=_vmem_limit(blocks, 0)),
        name="ple",
    )(x, g.reshape(1, D), pe, w_gate, w_proj)


def _layer(x, pe, w):
    B, S, D = x.shape
    T = B * S
    x2 = x.reshape(T, D)
    x2 = _ffn(x2, w["g_ffn1"], w["w1g"], w["w1u"], w["w1d"])

    u, gates, kt = _inproj(x2, w["g_mix"], w["w_in_t"], w["w_in_gate_t"], w["g_qn"], w["g_kn"])
    u = u.reshape(B, S, U_MAIN)
    ya = _attention(u, w["attn_bias"], B, S)

    n_chunks = S // ML_CHUNK
    gates_t = jnp.transpose(gates[:, :N_GATES].reshape(B, S, N_GATES), (0, 2, 1))
    gates_t = gates_t.reshape(B, N_GATES, n_chunks, ML_CHUNK)
    hf = _mlstm_sweep(u, kt, gates_t, w["b_igate"], w["b_fgate"], B, S, reverse=False)
    hm = _mlstm_sweep(u, kt, gates_t, w["b_igate"], w["b_fgate"], B, S, reverse=True, hf=hf)

    x2 = _outproj(x2, ya.reshape(T, W_A), hm.reshape(T, W_M), u.reshape(T, U_MAIN), w["g_mh"],
                  w["w_out_a"], w["w_out_m"])
    x2 = _ffn(x2, w["g_ffn2"], w["w2g"], w["w2u"], w["w2d"])
    x2 = _ple(x2, w["g_ple"], pe.reshape(T, -1), w["w_ple_gate"], w["w_ple_proj"])
    return x2.reshape(B, S, D)


def kernel(x_prompt, x_sample, p_prompt, p_sample, g_ffn1, w_ffn1_gate, w_ffn1_up, w_ffn1_down, g_mix, w_in, b_igate, b_fgate, g_qn, g_kn, rpb, g_mh, w_out, g_ffn2, w_ffn2_gate, w_ffn2_up, w_ffn2_down, g_ple, w_ple_gate, w_ple_proj):
    depth = g_ffn1.shape[0]
    xs = [x_prompt, x_sample]
    ps = [p_prompt, p_sample]
    for i in range(depth):
        w_in_t = jnp.transpose(w_in[i]).astype(BF16)
        u_end = 3 * W_A + 4 * W_M
        gate_rows = jnp.pad(w_in_t[u_end:], ((0, LANES - N_GATES), (0, 0)))
        w = {
            "g_ffn1": g_ffn1[i],
            "w1g": w_ffn1_gate[i].astype(BF16), "w1u": w_ffn1_up[i].astype(BF16),
            "w1d": w_ffn1_down[i],
            "g_mix": g_mix[i],
            "w_in_t": w_in_t,
            "w_in_gate_t": gate_rows,
            "b_igate": b_igate[i], "b_fgate": b_fgate[i], "g_qn": g_qn[i], "g_kn": g_kn[i],
            "attn_bias": _attn_bias_table(rpb[i]), "g_mh": g_mh[i],
            "w_out_a": w_out[i, :W_A].astype(BF16), "w_out_m": w_out[i, W_A:].astype(BF16),
            "g_ffn2": g_ffn2[i],
            "w2g": w_ffn2_gate[i].astype(BF16), "w2u": w_ffn2_up[i].astype(BF16),
            "w2d": w_ffn2_down[i],
            "g_ple": g_ple[i], "w_ple_gate": w_ple_gate[i].astype(BF16),
            "w_ple_proj": w_ple_proj[i].astype(BF16),
        }
        xs = [_layer(x, p[i], w) for x, p in zip(xs, ps)]
    return (xs[0], xs[1])
```

```python
import functools

import numpy as np
import jax
import jax.numpy as jnp
from jax import lax
from jax.experimental import pallas as pl
from jax.experimental.pallas import tpu as pltpu

F32 = jnp.float32
BF16 = jnp.bfloat16

GRID_W = 64
WIN_H = 8
WIN_W = 16
H_A = 8
DH_A = 128
H_M = 4
DH_M = 256
W_A = H_A * DH_A
W_M = H_M * DH_M
EPS = 1e-6
N_GATES = 4 * H_M
U_MAIN = 3 * W_A + 3 * W_M

LANES = 128
V7X_VMEM_BYTES = 64 * 1024 * 1024

TM = 512
TM_OUT = 1024
TM_FFN = 1024
TM_IN = 1024
TF = 512
TN_IN = 1024
ATT_RB = 4
ATT_WIN = 12
ATT_HEADS = 8
ML_CHUNK = 256
ML_BATCH = 2
NEG_BIG = -1e30


def _vmem_limit(block_bytes, scratch_bytes):
    est = 2 * block_bytes + scratch_bytes + 12 * 1024 * 1024
    return int(min(est, V7X_VMEM_BYTES - 6 * 1024 * 1024))


def _nbytes(shape, dtype):
    return int(np.prod(shape)) * jnp.dtype(dtype).itemsize


def _rms_rows(x, g):
    ms = jnp.mean(x * x, axis=-1, keepdims=True)
    return x * lax.rsqrt(ms + EPS) * g


def _ffn_kernel(x_ref, g_ref, wg_ref, wu_ref, wd_ref, o_ref, hn_ref):
    j = pl.program_id(1)

    def half_step(hn):
        a = jnp.dot(hn, wg_ref[...], preferred_element_type=F32)
        b = jnp.dot(hn, wu_ref[...], preferred_element_type=F32)
        h = (a * jax.nn.sigmoid(a) * (0.5 * b)).astype(BF16)
        return jnp.dot(h, wd_ref[...].astype(BF16), preferred_element_type=F32)

    @pl.when(j == 0)
    def _():
        x = x_ref[...]
        hn = _rms_rows(x, g_ref[...]).astype(BF16)
        hn_ref[...] = hn
        o_ref[...] = x + half_step(hn)

    @pl.when(j > 0)
    def _():
        o_ref[...] += half_step(hn_ref[...])


def _ffn(x, g, wg, wu, wd):
    T, D = x.shape
    F = wd.shape[0]
    tm = TM_FFN
    blocks = (_nbytes((tm, D), F32) * 2 + _nbytes((D, 2 * TF), BF16) + _nbytes((TF, D), F32))
    scratch = _nbytes((tm, D), BF16)
    return pl.pallas_call(
        _ffn_kernel,
        grid=(T // tm, F // TF),
        in_specs=[
            pl.BlockSpec((tm, D), lambda i, j: (i, 0)),
            pl.BlockSpec((1, D), lambda i, j: (0, 0)),
            pl.BlockSpec((D, TF), lambda i, j: (0, j)),
            pl.BlockSpec((D, TF), lambda i, j: (0, j)),
            pl.BlockSpec((TF, D), lambda i, j: (j, 0)),
        ],
        out_specs=pl.BlockSpec((tm, D), lambda i, j: (i, 0)),
        out_shape=jax.ShapeDtypeStruct((T, D), F32),
        scratch_shapes=[pltpu.VMEM((tm, D), BF16)],
        compiler_params=pltpu.CompilerParams(
            dimension_semantics=("arbitrary", "arbitrary"),
            vmem_limit_bytes=_vmem_limit(blocks, scratch)),
        name="ffn",
    )(x, g.reshape(1, D), wg, wu, wd)


_NT = (((1,), (1,)), ((), ()))


def _inproj_kernel(x_ref, g_ref, wt_ref, wgate_ref, gq_ref, gk_ref,
                   u_ref, gates_ref, kt_ref, hn_ref, *, n_u_tiles):
    j = pl.program_id(1)

    n_q_tiles = W_A // TN_IN
    n_qk_tiles = 2 * n_q_tiles

    def project(hn):
        return lax.dot_general(hn, wt_ref[...], _NT, preferred_element_type=F32)

    def store_qk_normed(acc, gain):
        for hh in range(TN_IN // DH_A):
            sl = slice(hh * DH_A, (hh + 1) * DH_A)
            u_ref[:, sl] = _rms_rows(acc[:, sl], gain).astype(BF16)

    @pl.when(j == 0)
    def _():
        hn = _rms_rows(x_ref[...], g_ref[...]).astype(BF16)
        hn_ref[...] = hn
        gates_ref[...] = lax.dot_general(hn, wgate_ref[...], _NT, preferred_element_type=F32)
        store_qk_normed(project(hn), gq_ref[...] * (DH_A ** -0.5))

    @pl.when((j > 0) & (j < n_qk_tiles))
    def _():
        gain = jnp.where(j < n_q_tiles, gq_ref[...] * (DH_A ** -0.5), gk_ref[...])
        store_qk_normed(project(hn_ref[...]), gain)

    @pl.when((j >= n_qk_tiles) & (j < n_u_tiles))
    def _():
        u_ref[...] = project(hn_ref[...]).astype(BF16)

    @pl.when(j == n_u_tiles)
    def _():
        kt = lax.dot_general(wt_ref[...], hn_ref[...], _NT, preferred_element_type=F32)
        kt_ref[...] = (kt * (DH_M ** -0.5)).astype(BF16)


def _inproj(x, g, w_t, w_gate_t, g_qn, g_kn):
    T, D = x.shape
    tm = TM_IN
    nu = U_MAIN // TN_IN
    assert W_M == TN_IN
    blocks = (_nbytes((tm, D), F32) + _nbytes((TN_IN, D), BF16) + _nbytes((LANES, D), BF16)
              + _nbytes((tm, TN_IN), BF16) + _nbytes((tm, LANES), F32) + _nbytes((W_M, tm), BF16))
    scratch = _nbytes((tm, D), BF16)
    last = nu - 1
    km_block = (3 * W_A + W_M) // TN_IN

    def w_row_block(j):
        return jnp.where(j < km_block, j, jnp.where(j < nu, j + 1, km_block))

    return pl.pallas_call(
        functools.partial(_inproj_kernel, n_u_tiles=nu),
        grid=(T // tm, nu + 1),
        in_specs=[
            pl.BlockSpec((tm, D), lambda i, j: (i, 0)),
            pl.BlockSpec((1, D), lambda i, j: (0, 0)),
            pl.BlockSpec((TN_IN, D), lambda i, j: (w_row_block(j), 0)),
            pl.BlockSpec((LANES, D), lambda i, j: (0, 0)),
            pl.BlockSpec((1, DH_A), lambda i, j: (0, 0)),
            pl.BlockSpec((1, DH_A), lambda i, j: (0, 0)),
        ],
        out_specs=[
            pl.BlockSpec((tm, TN_IN), lambda i, j: (i, jnp.minimum(j, last))),
            pl.BlockSpec((tm, LANES), lambda i, j: (i, 0)),
            pl.BlockSpec((W_M, tm), lambda i, j: (0, i)),
        ],
        out_shape=[
            jax.ShapeDtypeStruct((T, U_MAIN), BF16),
            jax.ShapeDtypeStruct((T, LANES), F32),
            jax.ShapeDtypeStruct((W_M, T), BF16),
        ],
        scratch_shapes=[pltpu.VMEM((tm, D), BF16)],
        compiler_params=pltpu.CompilerParams(
            dimension_semantics=("arbitrary", "arbitrary"),
            vmem_limit_bytes=_vmem_limit(blocks, scratch)),
        name="inproj",
    )(x, g.reshape(1, D), w_t, w_gate_t, g_qn.reshape(1, DH_A), g_kn.reshape(1, DH_A))


def _attn_bias_table(rpb):
    a = np.arange(ATT_RB)[:, None]
    ik = np.arange(ATT_WIN)[None, :]
    kh = WIN_H
    valid0 = ik < kh
    dr0 = ik - a + (WIN_H - 1)
    valid1 = (ik >= a) & (ik < a + kh)
    dr1 = ik - a + (WIN_H - 1) - kh // 2
    valid2 = (ik >= ATT_WIN - kh) & (ik < ATT_WIN)
    dr2 = (ik - ATT_WIN) - (a - ATT_RB) + (WIN_H - 1)
    valid = np.stack([np.broadcast_to(valid0, dr0.shape), valid1,
                      np.broadcast_to(valid2, dr0.shape)])
    dr = np.stack([dr0, dr1, dr2])

    c = np.arange(GRID_W)
    cs = np.clip(c - WIN_W // 2, 0, GRID_W - WIN_W)
    colmask = (c[None, :] >= cs[:, None]) & (c[None, :] < cs[:, None] + WIN_W)
    dc = np.clip(c[None, :] - c[:, None], -(WIN_W - 1), WIN_W - 1) + (WIN_W - 1)

    onehot = (dc[None] == np.arange(2 * WIN_W - 1)[:, None, None]).astype(np.float32)
    tiles = jnp.einsum("hrc,cqk->hqrk", rpb.astype(F32), jnp.asarray(onehot),
                       precision=lax.Precision.HIGHEST)
    tiles = jnp.where(jnp.asarray(colmask)[:, None, :], tiles, NEG_BIG)
    seq = tiles.reshape(H_A, GRID_W, (2 * WIN_H - 1) * GRID_W)
    classes = []
    for cls in range(3):
        rows = []
        for qa in range(ATT_RB):
            kk = np.nonzero(valid[cls, qa])[0]
            k0, k1 = int(kk[0]), int(kk[-1]) + 1
            d0 = int(dr[cls, qa, k0])
            assert np.array_equal(kk, np.arange(k0, k1))
            assert np.array_equal(dr[cls, qa, k0:k1], np.arange(d0, d0 + k1 - k0))
            strip = seq[:, :, d0 * GRID_W:(d0 + k1 - k0) * GRID_W]
            rows.append(jnp.pad(strip, ((0, 0), (0, 0), (k0 * GRID_W, (ATT_WIN - k1) * GRID_W)),
                                constant_values=NEG_BIG))
        classes.append(jnp.concatenate(rows, axis=1))
    return jnp.stack(classes)


def _attn_kernel(q_ref, k_ref, v_ref, bias_ref, o_ref, *, n_rows):
    i = pl.program_id(2)
    nb = n_rows // ATT_RB
    cls = jnp.where(i == 0, 0, jnp.where(i == nb - 1, 2, 1))

    def scores(hh):
        sl = slice(hh * DH_A, (hh + 1) * DH_A)
        q = q_ref[0, :, sl]
        kw = k_ref[0, :, sl]
        s = lax.dot_general(q, kw, (((1,), (1,)), ((), ())), preferred_element_type=F32)
        s = s + bias_ref[cls, hh]
        return s, jnp.max(s, axis=-1, keepdims=True)

    def weighted_values(hh, s, m):
        sl = slice(hh * DH_A, (hh + 1) * DH_A)
        p = jnp.exp(s - m)
        l = jnp.sum(p, axis=-1, keepdims=True)
        o = jnp.dot(p.astype(BF16), v_ref[0, :, sl], preferred_element_type=F32)
        o_ref[0, :, sl] = (o / l).astype(BF16)

    pending = None
    for hh in range(ATT_HEADS):
        cur = scores(hh)
        if pending is not None:
            weighted_values(*pending)
        pending = (hh, *cur)
    weighted_values(*pending)


def _attention(u, bias, B, S):
    R = S // GRID_W
    assert R % ATT_RB == 0 and R >= ATT_WIN and R // ATT_RB >= 3
    tq = ATT_RB * GRID_W
    tk = ATT_WIN * GRID_W
    hw = ATT_HEADS * DH_A
    ng = H_A // ATT_HEADS
    blocks = (_nbytes((tq, hw), BF16) * 2 + _nbytes((tk, hw), BF16) * 2
              + _nbytes((3, ATT_HEADS, tq, tk), F32))

    def win_start(i):
        return jnp.clip(i * ATT_RB - WIN_H // 2, 0, R - ATT_WIN) * GRID_W

    return pl.pallas_call(
        functools.partial(_attn_kernel, n_rows=R),
        grid=(ng, B, R // ATT_RB),
        in_specs=[
            pl.BlockSpec((1, tq, hw), lambda h, b, i: (b, i, h)),
            pl.BlockSpec((pl.Element(1), pl.Element(tk), pl.Element(hw)),
                         lambda h, b, i: (b, win_start(i), (ng + h) * hw)),
            pl.BlockSpec((pl.Element(1), pl.Element(tk), pl.Element(hw)),
                         lambda h, b, i: (b, win_start(i), (2 * ng + h) * hw)),
            pl.BlockSpec((3, ATT_HEADS, tq, tk), lambda h, b, i: (0, h, 0, 0)),
        ],
        out_specs=pl.BlockSpec((1, tq, hw), lambda h, b, i: (b, i, h)),
        out_shape=jax.ShapeDtypeStruct((B, S, W_A), BF16),
        compiler_params=pltpu.CompilerParams(
            dimension_semantics=("arbitrary", "arbitrary", "arbitrary"),
            vmem_limit_bytes=_vmem_limit(blocks, 0)),
        name="attention",
    )(u, u, u, bias)


def _mlstm_kernel(*refs, reverse, n_chunks):
    bi_ref, bf_ref, q_ref = refs[:3]
    kt_refs = refs[3:3 + ML_BATCH]
    rest = refs[3 + ML_BATCH:]
    if reverse:
        (v_ref, gi_ref, gf_ref, hf_ref, og_ref, gmh_ref,
         o_ref, ct_ref, m_ref, b_ref, e_ref) = rest
    else:
        (v_ref, gi_ref, gf_ref, o_ref, ct_ref, m_ref, b_ref, e_ref) = rest
    L = ML_CHUNK
    c = pl.program_id(1)
    d = 1 if reverse else 0

    row = lax.broadcasted_iota(jnp.int32, (L, L), 0)
    col = lax.broadcasted_iota(jnp.int32, (L, L), 1)
    tri = (col >= row) if reverse else (col <= row)
    eye = row == col

    @pl.when(c == 0)
    def _():
        ct_ref[...] = jnp.zeros_like(ct_ref)
        m_ref[...] = jnp.zeros_like(m_ref)
        cum = jnp.where((row >= col) if reverse else (row <= col), 1.0, 0.0).astype(F32)
        for bb in range(ML_BATCH):
            for hh in range(H_M):
                ig = gi_ref[bb, hh] + bi_ref[d, hh]
                lf = jax.nn.log_sigmoid(gf_ref[bb, hh] + bf_ref[d, hh])
                b = jnp.dot(lf, cum, preferred_element_type=F32, precision=lax.Precision.HIGHEST)
                b_ref[bb * H_M + hh] = b
                e_ref[bb * H_M + hh] = ig - b

    cc = (n_chunks - 1 - c) if reverse else c

    def chain(ch):
        bb, hh = divmod(ch, H_M)
        sl = slice(hh * DH_M, (hh + 1) * DH_M)
        b_row = b_ref[ch, pl.ds(cc, 1), :]
        e_row = e_ref[ch, pl.ds(cc, 1), :]
        b_col = jnp.sum(jnp.where(eye, b_row, 0.0), axis=1, keepdims=True)
        e_col = jnp.sum(jnp.where(eye, e_row, 0.0), axis=1, keepdims=True)
        a_tot = b_row[:, 0:1] if reverse else b_row[:, L - 1:L]
        m_prev = m_ref[ch]
        yield

        q = q_ref[bb, :, sl]
        kt = kt_refs[bb][sl, :]
        v = v_ref[bb, :, sl]

        dmat = jnp.where(tri, b_col + e_row, -jnp.inf)
        inter = b_col + m_prev
        m_t = jnp.maximum(inter, jnp.max(dmat, axis=1, keepdims=True))
        s = jnp.dot(q, kt, preferred_element_type=F32)
        ct = ct_ref[ch]
        inter_aug = jnp.dot(q, ct.astype(BF16), preferred_element_type=F32)
        yield

        dexp = jnp.exp(dmat - m_t)
        sqk = s * dexp
        sc = jnp.exp(inter - m_t)
        yield

        num = sc * inter_aug[:, :DH_M] + jnp.dot(sqk.astype(BF16), v, preferred_element_type=F32)
        den = sc * inter_aug[:, DH_M:] + jnp.sum(sqk, axis=1, keepdims=True)
        rden = 1.0 / jnp.maximum(jnp.abs(den), jnp.exp(-m_t))
        hdir = num * jnp.concatenate([rden] * (DH_M // LANES), axis=1)

        if reverse:
            hm = _rms_rows(hf_ref[bb, :, sl] + hdir, gmh_ref[:, sl])
            o_ref[bb, :, sl] = (jax.nn.sigmoid(og_ref[bb, :, sl].astype(F32)) * hm).astype(BF16)
        else:
            o_ref[bb, :, sl] = hdir
        yield

        g_col = a_tot + e_col
        m_loc = jnp.max(g_col, axis=0, keepdims=True)
        m_new = jnp.maximum(a_tot + m_prev, m_loc)
        s_prev = jnp.exp(a_tot + m_prev - m_new)
        w_col = jnp.exp(g_col - m_new)
        vw_aug = jnp.concatenate(
            [(v.astype(F32) * w_col).astype(BF16),
             jnp.broadcast_to(w_col, (L, LANES)).astype(BF16)], axis=1)
        ct_ref[ch] = s_prev * ct + jnp.dot(kt, vw_aug, preferred_element_type=F32)
        m_ref[ch] = m_new

    chains = [chain(ch) for ch in range(ML_BATCH * H_M)]
    while chains:
        chains = [g for g in chains if next(g, StopIteration) is not StopIteration]


def _mlstm_sweep(u, kt, gates_t, b_igate, b_fgate, B, S, *, reverse, hf=None, g_mh=None):
    L = ML_CHUNK
    N = S // L
    d = 1 if reverse else 0
    qoff = 3 * W_A // W_M
    cidx = (lambda c: N - 1 - c) if reverse else (lambda c: c)
    smem = pl.BlockSpec(memory_space=pltpu.SMEM)
    nb = ML_BATCH
    assert B % nb == 0
    nch = nb * H_M
    kt_specs = [pl.BlockSpec((W_M, L), functools.partial(
        lambda b, c, r: (0, (nb * b + r) * N + cidx(c)), r=r)) for r in range(nb)]
    in_specs = [
        smem, smem,
        pl.BlockSpec((nb, L, W_M), lambda b, c: (b, cidx(c), qoff)),
        *kt_specs,
        pl.BlockSpec((nb, L, W_M), lambda b, c: (b, cidx(c), qoff + 1)),
        pl.BlockSpec((nb, H_M, N, L), lambda b, c: (b, d, 0, 0)),
        pl.BlockSpec((nb, H_M, N, L), lambda b, c: (b, 2 + d, 0, 0)),
    ]
    args = [b_igate, b_fgate, u, *([kt] * nb), u, gates_t, gates_t]
    if reverse:
        in_specs += [
            pl.BlockSpec((nb, L, W_M), lambda b, c: (b, cidx(c), 0)),
            pl.BlockSpec((nb, L, W_M), lambda b, c: (b, cidx(c), qoff + 2)),
            pl.BlockSpec((1, W_M), lambda b, c: (0, 0)),
        ]
        args += [hf, u, g_mh.reshape(1, W_M)]
        out_dtype = BF16
    else:
        out_dtype = F32
    blocks = nb * (_nbytes((L, W_M), BF16) * 4 + _nbytes((H_M, N, L), F32) * 2
                   + _nbytes((L, W_M), F32) * 2)
    scratch = (nch * _nbytes((DH_M, DH_M + LANES), F32) + 2 * nch * _nbytes((N, L), F32)
               + nch * 8 * LANES * 4)
    return pl.pallas_call(
        functools.partial(_mlstm_kernel, reverse=reverse, n_chunks=N),
        grid=(B // nb, N),
        in_specs=in_specs,
        out_specs=pl.BlockSpec((nb, L, W_M), lambda b, c: (b, cidx(c), 0)),
        out_shape=jax.ShapeDtypeStruct((B, S, W_M), out_dtype),
        scratch_shapes=[
            pltpu.VMEM((nch, DH_M, DH_M + LANES), F32),
            pltpu.VMEM((nch, 1, 1), F32),
            pltpu.VMEM((nch, N, L), F32),
            pltpu.VMEM((nch, N, L), F32),
        ],
        compiler_params=pltpu.CompilerParams(
            dimension_semantics=("arbitrary", "arbitrary"),
            vmem_limit_bytes=_vmem_limit(blocks, scratch)),
        name="mlstm_bwd" if reverse else "mlstm_fwd",
    )(*args)


def _outproj_kernel(x_ref, ya_ref, ym_ref, wa_ref, wm_ref, o_ref):
    o_ref[...] = (x_ref[...]
                  + jnp.dot(ya_ref[...], wa_ref[...], preferred_element_type=F32)
                  + jnp.dot(ym_ref[...], wm_ref[...], preferred_element_type=F32))


def _outproj(x, ya, ym, w_a, w_m):
    T, D = x.shape
    tm = TM_OUT
    once = pl.Buffered(1)
    blocks = (_nbytes((tm, D), F32) * 2 + _nbytes((tm, W_A), BF16) * 2 + _nbytes((W_A, D), BF16))
    return pl.pallas_call(
        _outproj_kernel,
        grid=(T // tm,),
        in_specs=[
            pl.BlockSpec((tm, D), lambda i: (i, 0)),
            pl.BlockSpec((tm, W_A), lambda i: (i, 0)),
            pl.BlockSpec((tm, W_M), lambda i: (i, 0)),
            pl.BlockSpec((W_A, D), lambda i: (0, 0), pipeline_mode=once),
            pl.BlockSpec((W_M, D), lambda i: (0, 0), pipeline_mode=once),
        ],
        out_specs=pl.BlockSpec((tm, D), lambda i: (i, 0)),
        out_shape=jax.ShapeDtypeStruct((T, D), F32),
        compiler_params=pltpu.CompilerParams(
            dimension_semantics=("arbitrary",),
            vmem_limit_bytes=_vmem_limit(blocks, 0)),
        name="outproj",
    )(x, ya, ym, w_a, w_m)


def _ple_kernel(x_ref, g_ref, pe_ref, wg_ref, wp_ref, o_ref):
    x = x_ref[...]
    hn = _rms_rows(x, g_ref[...]).astype(BF16)
    gate = jax.nn.sigmoid(jnp.dot(hn, wg_ref[...], preferred_element_type=F32))
    proj = jnp.dot(pe_ref[...].astype(BF16), wp_ref[...], preferred_element_type=F32)
    o_ref[...] = x + gate * proj


def _ple(x, g, pe, w_gate, w_proj):
    T, D = x.shape
    P = pe.shape[1]
    blocks = (_nbytes((TM, D), F32) * 2 + _nbytes((TM, P), F32) + _nbytes((D, D), BF16)
              + _nbytes((P, D), BF16))
    return pl.pallas_call(
        _ple_kernel,
        grid=(T // TM,),
        in_specs=[
            pl.BlockSpec((TM, D), lambda i: (i, 0)),
            pl.BlockSpec((1, D), lambda i: (0, 0)),
            pl.BlockSpec((TM, P), lambda i: (i, 0)),
            pl.BlockSpec((D, D), lambda i: (0, 0)),
            pl.BlockSpec((P, D), lambda i: (0, 0)),
        ],
        out_specs=pl.BlockSpec((TM, D), lambda i: (i, 0)),
        out_shape=jax.ShapeDtypeStruct((T, D), F32),
        compiler_params=pltpu.CompilerParams(
            dimension_semantics=("arbitrary",),
            vmem_limit_bytes=_vmem_limit(blocks, 0)),
        name="ple",
    )(x, g.reshape(1, D), pe, w_gate, w_proj)


def _layer(x, pe, w):
    B, S, D = x.shape
    T = B * S
    x2 = x.reshape(T, D)
    x2 = _ffn(x2, w["g_ffn1"], w["w1g"], w["w1u"], w["w1d"])

    u, gates, kt = _inproj(x2, w["g_mix"], w["w_in_t"], w["w_in_gate_t"], w["g_qn"], w["g_kn"])
    u = u.reshape(B, S, U_MAIN)
    ya = _attention(u, w["attn_bias"], B, S)

    n_chunks = S // ML_CHUNK
    gates_t = jnp.transpose(gates[:, :N_GATES].reshape(B, S, N_GATES), (0, 2, 1))
    gates_t = gates_t.reshape(B, N_GATES, n_chunks, ML_CHUNK)
    hf = _mlstm_sweep(u, kt, gates_t, w["b_igate"], w["b_fgate"], B, S, reverse=False)
    ym = _mlstm_sweep(u, kt, gates_t, w["b_igate"], w["b_fgate"], B, S, reverse=True,
                      hf=hf, g_mh=w["g_mh"])

    x2 = _outproj(x2, ya.reshape(T, W_A), ym.reshape(T, W_M), w["w_out_a"], w["w_out_m"])
    x2 = _ffn(x2, w["g_ffn2"], w["w2g"], w["w2u"], w["w2d"])
    x2 = _ple(x2, w["g_ple"], pe.reshape(T, -1), w["w_ple_gate"], w["w_ple_proj"])
    return x2.reshape(B, S, D)


def kernel(x_prompt, x_sample, p_prompt, p_sample, g_ffn1, w_ffn1_gate, w_ffn1_up, w_ffn1_down, g_mix, w_in, b_igate, b_fgate, g_qn, g_kn, rpb, g_mh, w_out, g_ffn2, w_ffn2_gate, w_ffn2_up, w_ffn2_down, g_ple, w_ple_gate, w_ple_proj):
    depth = g_ffn1.shape[0]
    xs = [x_prompt, x_sample]
    ps = [p_prompt, p_sample]
    for i in range(depth):
        w_in_t = jnp.transpose(w_in[i]).astype(BF16)
        u_end = 3 * W_A + 4 * W_M
        gate_rows = jnp.pad(w_in_t[u_end:], ((0, LANES - N_GATES), (0, 0)))
        w = {
            "g_ffn1": g_ffn1[i],
            "w1g": w_ffn1_gate[i].astype(BF16), "w1u": w_ffn1_up[i].astype(BF16),
            "w1d": w_ffn1_down[i],
            "g_mix": g_mix[i],
            "w_in_t": w_in_t,
            "w_in_gate_t": gate_rows,
            "b_igate": b_igate[i], "b_fgate": b_fgate[i], "g_qn": g_qn[i], "g_kn": g_kn[i],
            "attn_bias": _attn_bias_table(rpb[i]), "g_mh": g_mh[i],
            "w_out_a": w_out[i, :W_A].astype(BF16), "w_out_m": w_out[i, W_A:].astype(BF16),
            "g_ffn2": g_ffn2[i],
            "w2g": w_ffn2_gate[i].astype(BF16), "w2u": w_ffn2_up[i].astype(BF16),
            "w2d": w_ffn2_down[i],
            "g_ple": g_ple[i], "w_ple_gate": w_ple_gate[i].astype(BF16),
            "w_ple_proj": w_ple_proj[i].astype(BF16),
        }
        xs = [_layer(x, p[i], w) for x, p in zip(xs, ps)]
    return (xs[0], xs[1])
```
